```python
import jax, jax.numpy as jnp
from jax import lax
import numpy as np

D_MODEL = 1024
BATCH = 4
SEQ = 8192
DEPTH = 1
DEC_BATCH = 128
DEC_SEQ = 1
PAST_LEN = 16384
PAGE_SIZE = 128

N_HEADS = 16
N_KV_HEADS = 4
HEAD_DIM = 64
GROUP = N_HEADS // N_KV_HEADS
WINDOW = 128
BLOCK = WINDOW
ROPE_THETA = 10000.0
C_CONV = D_MODEL
CONV_W = 31
D_FF = 4 * D_MODEL
D_PLE = 256
EPS = 1e-6
NEG = -1e30
Q_W = N_HEADS * HEAD_DIM
KV_W = N_KV_HEADS * HEAD_DIM
D_IN = Q_W + 2 * KV_W + 2 * C_CONV + 2 * D_MODEL

kernel_name = "hybrid_conformer_swa_sink_decoder_step"


def rmsnorm(x, g):
    xf = x.astype(jnp.float32)
    y = xf * lax.rsqrt(jnp.mean(xf * xf, axis=-1, keepdims=True) + EPS)
    return (y * g.astype(jnp.float32)).astype(x.dtype)


def layernorm(x, g, b):
    xf = x.astype(jnp.float32)
    mu = jnp.mean(xf, axis=-1, keepdims=True)
    var = jnp.mean(jnp.square(xf - mu), axis=-1, keepdims=True)
    y = (xf - mu) * lax.rsqrt(var + EPS) * g.astype(jnp.float32) + b.astype(jnp.float32)
    return y.astype(x.dtype)


def rope(x, pos):
    half = HEAD_DIM // 2
    inv = jnp.power(jnp.float32(ROPE_THETA), -jnp.arange(half, dtype=jnp.float32) / half)
    ang = pos.astype(jnp.float32)[:, None] * inv[None, :]
    cos = jnp.cos(ang)[None, :, None, :]
    sin = jnp.sin(ang)[None, :, None, :]
    xf = x.astype(jnp.float32)
    x1, x2 = xf[..., :half], xf[..., half:]
    return jnp.concatenate([x1 * cos - x2 * sin, x2 * cos + x1 * sin], axis=-1).astype(x.dtype)


def sink_attention(q, k, v, mask, sinks):
    s = jnp.einsum('bnqhgd,bnkhd->bnhgqk', q, k, preferred_element_type=jnp.float32) * (HEAD_DIM ** -0.5)
    s = jnp.where(mask[None, :, None, None], s, NEG)
    sk = sinks.astype(jnp.float32).reshape(N_KV_HEADS, GROUP)[None, None, :, :, None, None]
    m = jnp.maximum(jnp.max(s, axis=-1, keepdims=True), sk)
    e = jnp.exp(s - m)
    den = jnp.sum(e, axis=-1, keepdims=True) + jnp.exp(sk - m)
    pr = (e / den).astype(v.dtype)
    return jnp.einsum('bnhgqk,bnkhd->bnqhgd', pr, v)


def branch_inputs(x, pos, lp):
    B, T = x.shape[0], x.shape[1]
    u = rmsnorm(x, lp['ln1'])
    z = u @ lp['w_in']
    q, k, v, glu, gts = jnp.split(z, [Q_W, Q_W + KV_W, Q_W + 2 * KV_W, Q_W + 2 * KV_W + 2 * C_CONV], axis=-1)
    q = rope(rmsnorm(q.reshape(B, T, N_HEADS, HEAD_DIM), lp['q_norm']), pos)
    k = rope(rmsnorm(k.reshape(B, T, N_KV_HEADS, HEAD_DIM), lp['k_norm']), pos)
    v = v.reshape(B, T, N_KV_HEADS, HEAD_DIM)
    glu = glu + lp['b_glu']
    a = glu[..., :C_CONV] * jax.nn.sigmoid(glu[..., C_CONV:])
    return q, k, v, a, gts


def conv_branch(a_hist, lp):
    y = lax.conv_general_dilated(a_hist, lp['conv_dw'][:, None, :], (1,), 'VALID',
                                 dimension_numbers=('NWC', 'WIO', 'NWC'),
                                 feature_group_count=C_CONV) + lp['conv_dw_b']
    y = jax.nn.silu(layernorm(y, lp['conv_ln_g'], lp['conv_ln_b']))
    return y @ lp['w_conv_out'] + lp['b_conv_out']


def finish(x, attn_o, conv_o, gts, p, lp):
    g_attn, g_conv = jnp.split(gts, 2, axis=-1)
    mixed = jax.nn.sigmoid(g_attn) * (attn_o @ lp['w_o_attn']) + jax.nn.sigmoid(g_conv) * conv_o
    h = x + mixed @ lp['w_out']
    h = h + jnp.square(jax.nn.relu(rmsnorm(h, lp['ln2']) @ lp['w_ff1'])) @ lp['w_ff2']
    gate = jax.nn.sigmoid(rmsnorm(h, lp['ln_ple']) @ lp['w_ple_gate'])
    return h + gate * (p @ lp['w_ple'])


def prompt_layer(x, p, lp):
    B, S = x.shape[0], x.shape[1]
    pos = jnp.arange(S, dtype=jnp.int32)
    q, k, v, a, gts = branch_inputs(x, pos, lp)
    nb = S // BLOCK
    qb = q.reshape(B, nb, BLOCK, N_KV_HEADS, GROUP, HEAD_DIM)
    kb = k.reshape(B, nb, BLOCK, N_KV_HEADS, HEAD_DIM)
    vb = v.reshape(B, nb, BLOCK, N_KV_HEADS, HEAD_DIM)
    shift = ((0, 0), (1, 0), (0, 0), (0, 0), (0, 0))
    kc = jnp.concatenate([jnp.pad(kb[:, :-1], shift), kb], axis=2)
    vc = jnp.concatenate([jnp.pad(vb[:, :-1], shift), vb], axis=2)
    i = jnp.arange(BLOCK)[:, None]
    j = jnp.arange(2 * BLOCK)[None, :]
    rel = i - j + BLOCK
    band = (rel >= 0) & (rel < WINDOW)
    mask = band[None] & ((jnp.arange(nb)[:, None, None] > 0) | (j[None] >= BLOCK))
    attn_o = sink_attention(qb, kc, vc, mask, lp['sinks']).reshape(B, S, Q_W)
    a_hist = jnp.pad(a, ((0, 0), (CONV_W - 1, 0), (0, 0)))
    conv_o = conv_branch(a_hist, lp)
    y = finish(x, attn_o, conv_o, gts, p, lp)
    wb = min(WINDOW, S)
    return y, k[:, S - wb:], v[:, S - wb:], a[:, S - (CONV_W - 1):]


def sample_layer(x, p, ck, cv, cs, lp):
    Bd, T = x.shape[0], x.shape[1]
    wb = ck.shape[1]
    pos = PAST_LEN + jnp.arange(T, dtype=jnp.int32)
    q, k, v, a, gts = branch_inputs(x, pos, lp)
    kc = jnp.concatenate([ck.astype(k.dtype), k], axis=1)
    vc = jnp.concatenate([cv.astype(v.dtype), v], axis=1)
    kpos = PAST_LEN - wb + jnp.arange(wb + T, dtype=jnp.int32)
    rel = pos[:, None] - kpos[None, :]
    mask = ((rel >= 0) & (rel < WINDOW) & (kpos[None, :] >= 0))[None]
    qb = q.reshape(Bd, 1, T, N_KV_HEADS, GROUP, HEAD_DIM)
    attn_o = sink_attention(qb, kc[:, None], vc[:, None], mask, lp['sinks']).reshape(Bd, T, Q_W)
    a_hist = jnp.concatenate([cs.astype(a.dtype), a], axis=1)
    conv_o = conv_branch(a_hist, lp)
    y = finish(x, attn_o, conv_o, gts, p, lp)
    return y, kc[:, T:], vc[:, T:], a_hist[:, T:]


def setup_inputs(seed: int = 0) -> dict:
    key = jax.random.key(seed)
    ks = iter(jax.random.split(key, 40))
    f32 = jnp.float32

    def nrm(shape, scale):
        return jax.random.normal(next(ks), shape, f32) * scale

    L = DEPTH
    wb = min(WINDOW, PAST_LEN)
    return {
        "x_prompt": nrm((BATCH, SEQ, D_MODEL), 1.0),
        "x_sample": nrm((DEC_BATCH, DEC_SEQ, D_MODEL), 1.0),
        "cache_k": nrm((L, DEC_BATCH, wb, N_KV_HEADS, HEAD_DIM), 1.0),
        "cache_v": nrm((L, DEC_BATCH, wb, N_KV_HEADS, HEAD_DIM), 1.0),
        "state_conv": nrm((L, DEC_BATCH, CONV_W - 1, C_CONV), 0.5),
        "p_prompt": nrm((L, BATCH, SEQ, D_PLE), 1.0),
        "p_sample": nrm((L, DEC_BATCH, DEC_SEQ, D_PLE), 1.0),
        "ln1": 1.0 + nrm((L, D_MODEL), 0.02),
        "w_in": nrm((L, D_MODEL, D_IN), D_MODEL ** -0.5),
        "b_glu": nrm((L, 2 * C_CONV), 0.02),
        "q_norm": 1.0 + nrm((L, HEAD_DIM), 0.02),
        "k_norm": 1.0 + nrm((L, HEAD_DIM), 0.02),
        "sinks": nrm((L, N_HEADS), 0.5),
        "w_o_attn": nrm((L, Q_W, D_MODEL), Q_W ** -0.5),
        "conv_dw": nrm((L, CONV_W, C_CONV), CONV_W ** -0.5),
        "conv_dw_b": nrm((L, C_CONV), 0.02),
        "conv_ln_g": 1.0 + nrm((L, C_CONV), 0.02),
        "conv_ln_b": nrm((L, C_CONV), 0.02),
        "w_conv_out": nrm((L, C_CONV, D_MODEL), C_CONV ** -0.5),
        "b_conv_out": nrm((L, D_MODEL), 0.02),
        "w_out": nrm((L, D_MODEL, D_MODEL), D_MODEL ** -0.5),
        "ln2": 1.0 + nrm((L, D_MODEL), 0.02),
        "w_ff1": nrm((L, D_MODEL, D_FF), D_MODEL ** -0.5),
        "w_ff2": nrm((L, D_FF, D_MODEL), D_FF ** -0.5),
        "ln_ple": 1.0 + nrm((L, D_MODEL), 0.02),
        "w_ple_gate": nrm((L, D_MODEL, D_MODEL), D_MODEL ** -0.5),
        "w_ple": nrm((L, D_PLE, D_MODEL), D_PLE ** -0.5),
    }


def reference(x_prompt, x_sample, cache_k, cache_v, state_conv, p_prompt, p_sample,
              ln1, w_in, b_glu, q_norm, k_norm, sinks, w_o_attn, conv_dw, conv_dw_b,
              conv_ln_g, conv_ln_b, w_conv_out, b_conv_out, w_out, ln2, w_ff1, w_ff2,
              ln_ple, w_ple_gate, w_ple):
    xp, xs = x_prompt, x_sample
    kp_l, vp_l, cp_l, ks_l, vs_l, cs_l = [], [], [], [], [], []
    for i in range(DEPTH):
        lp = dict(ln1=ln1[i], w_in=w_in[i], b_glu=b_glu[i], q_norm=q_norm[i], k_norm=k_norm[i],
                  sinks=sinks[i], w_o_attn=w_o_attn[i], conv_dw=conv_dw[i], conv_dw_b=conv_dw_b[i],
                  conv_ln_g=conv_ln_g[i], conv_ln_b=conv_ln_b[i], w_conv_out=w_conv_out[i],
                  b_conv_out=b_conv_out[i], w_out=w_out[i], ln2=ln2[i], w_ff1=w_ff1[i],
                  w_ff2=w_ff2[i], ln_ple=ln_ple[i], w_ple_gate=w_ple_gate[i], w_ple=w_ple[i])
        xp, kp, vp, cp = prompt_layer(xp, p_prompt[i], lp)
        xs, ks_, vs_, cs_ = sample_layer(xs, p_sample[i], cache_k[i], cache_v[i], state_conv[i], lp)
        kp_l.append(kp); vp_l.append(vp); cp_l.append(cp)
        ks_l.append(ks_); vs_l.append(vs_); cs_l.append(cs_)
    new_k_prompt = jnp.stack(kp_l)
    new_v_prompt = jnp.stack(vp_l)
    new_conv_prompt = jnp.stack(cp_l)
    new_k_sample = jnp.stack(ks_l)
    new_v_sample = jnp.stack(vs_l)
    new_conv_sample = jnp.stack(cs_l)
    return (xp, xs, new_k_prompt, new_v_prompt, new_conv_prompt, new_k_sample, new_v_sample, new_conv_sample)
```

```python
import functools

import jax
import jax.numpy as jnp
from jax import lax
from jax.experimental import pallas as pl
from jax.experimental.pallas import tpu as pltpu

D_MODEL = 1024
N_HEADS = 16
N_KV_HEADS = 4
HEAD_DIM = 64
GROUP = N_HEADS // N_KV_HEADS
WINDOW = 128
ROPE_THETA = 10000.0
PAST_LEN = 16384
C_CONV = D_MODEL
CONV_W = 31
D_FF = 4 * D_MODEL
EPS = 1e-6
NEG = -1e30
Q_W = N_HEADS * HEAD_DIM
KV_W = N_KV_HEADS * HEAD_DIM
D_IN = Q_W + 2 * KV_W + 2 * C_CONV + 2 * D_MODEL

LANES = 128
MXU_N = 256
HALO = 32
VMEM_LIMIT = 56 * 1024 * 1024

F32 = jnp.float32
BF16 = jnp.bfloat16


def _dot(a, b):
    return jnp.dot(a, b, preferred_element_type=F32)


def _dot_nt(a, b):
    return lax.dot_general(a, b, (((1,), (1,)), ((), ())), preferred_element_type=F32)


def _const_spec(shape):
    nd = len(shape)
    return pl.BlockSpec(shape, lambda *_: (0,) * nd, pipeline_mode=pl.Buffered(1))


def _rms_scale(x):
    return lax.rsqrt(jnp.mean(x * x, axis=-1, keepdims=True) + EPS)


def _head_norm_rope(z, gain, seg, cos, sin_signed, scale):
    tm, width = z.shape
    outs = []
    lane = lax.broadcasted_iota(jnp.int32, (tm, LANES), 1)
    first_half = (lane & (HEAD_DIM - 1)) < (HEAD_DIM // 2)
    for c0 in range(0, width, MXU_N):
        zc = z[:, c0:c0 + MXU_N]
        ss = _dot((zc * zc).astype(BF16), seg)
        zn = zc * lax.rsqrt(ss * (1.0 / HEAD_DIM) + EPS) * gain[:, c0:c0 + MXU_N]
        for l0 in range(0, MXU_N, LANES):
            xc = zn[:, l0:l0 + LANES]
            partner = jnp.where(first_half,
                                pltpu.roll(xc, LANES - HEAD_DIM // 2, axis=1),
                                pltpu.roll(xc, HEAD_DIM // 2, axis=1))
            outs.append((xc * cos + partner * sin_signed) * scale)
    return outs


def _inproj_kernel(x_ref, ln1_ref, w_ref, bglu_ref, qg_ref, kg_ref, cos_ref, sin_ref, seg_ref,
                   q_ref, k_ref, v_ref, a_ref, g_ref, u_ref):
    x = x_ref[...]
    u_ref[...] = (x * _rms_scale(x) * ln1_ref[...]).astype(BF16)
    cos = cos_ref[...]
    sin = sin_ref[...]
    seg = seg_ref[...]

    zq = _dot(u_ref[...],w_ref[:, 0:Q_W])
    for i, qc in enumerate(_head_norm_rope(zq, qg_ref[...], seg, cos, sin, HEAD_DIM ** -0.5)):
        q_ref[:, i * LANES:(i + 1) * LANES] = qc.astype(BF16)

    zk = _dot(u_ref[...],w_ref[:, Q_W:Q_W + KV_W])
    for i, kc in enumerate(_head_norm_rope(zk, kg_ref[...], seg, cos, sin, 1.0)):
        k_ref[:, i * LANES:(i + 1) * LANES] = kc

    v_ref[...] = _dot(u_ref[...],w_ref[:, Q_W + KV_W:Q_W + 2 * KV_W])

    glu0 = Q_W + 2 * KV_W
    for c0 in range(0, C_CONV, MXU_N):
        lin = _dot(u_ref[...],w_ref[:, glu0 + c0:glu0 + c0 + MXU_N]) + bglu_ref[:, c0:c0 + MXU_N]
        gate = (_dot(u_ref[...],w_ref[:, glu0 + C_CONV + c0:glu0 + C_CONV + c0 + MXU_N])
                + bglu_ref[:, C_CONV + c0:C_CONV + c0 + MXU_N])
        a_ref[:, c0:c0 + MXU_N] = lin * jax.nn.sigmoid(gate)

    g0 = glu0 + 2 * C_CONV
    for c0 in range(0, 2 * D_MODEL, MXU_N):
        g_ref[:, c0:c0 + MXU_N] = jax.nn.sigmoid(_dot(u_ref[...],w_ref[:, g0 + c0:g0 + c0 + MXU_N])).astype(BF16)


def _inproj(x2d, cos, sin, pos_tiles, tm, ln1, w_in, b_glu, qg, kg, seg):
    n = x2d.shape[0]
    grid = (n // tm,)
    row = lambda i: (i, 0)
    pos = lambda i: (i % pos_tiles, 0)
    return pl.pallas_call(
        _inproj_kernel,
        grid=grid,
        in_specs=[
            pl.BlockSpec((tm, D_MODEL), row),
            _const_spec((1, D_MODEL)),
            _const_spec((D_MODEL, D_IN)),
            _const_spec((1, 2 * C_CONV)),
            _const_spec((1, Q_W)),
            _const_spec((1, KV_W)),
            pl.BlockSpec((tm, LANES), pos),
            pl.BlockSpec((tm, LANES), pos),
            _const_spec((MXU_N, MXU_N)),
        ],
        out_specs=[
            pl.BlockSpec((tm, Q_W), row),
            pl.BlockSpec((tm, KV_W), row),
            pl.BlockSpec((tm, KV_W), row),
            pl.BlockSpec((tm, C_CONV), row),
            pl.BlockSpec((tm, 2 * D_MODEL), row),
        ],
        out_shape=[
            jax.ShapeDtypeStruct((n, Q_W), BF16),
            jax.ShapeDtypeStruct((n, KV_W), F32),
            jax.ShapeDtypeStruct((n, KV_W), F32),
            jax.ShapeDtypeStruct((n, C_CONV), F32),
            jax.ShapeDtypeStruct((n, 2 * D_MODEL), BF16),
        ],
        scratch_shapes=[pltpu.VMEM((tm, D_MODEL), BF16)],
        compiler_params=pltpu.CompilerParams(dimension_semantics=("arbitrary",),
                                             vmem_limit_bytes=VMEM_LIMIT),
        name="in_proj",
    )(x2d, ln1, w_in, b_glu, qg, kg, cos, sin, seg)


def _place_halves(xc, parity):
    lane = lax.broadcasted_iota(jnp.int32, xc.shape, 1)
    low = lane < HEAD_DIM
    rolled = pltpu.roll(xc, HEAD_DIM, axis=1)
    zero = jnp.zeros_like(xc)
    if parity == 0:
        return jnp.where(low, xc, zero), jnp.where(low, zero, rolled)
    return jnp.where(low, rolled, zero), jnp.where(low, zero, xc)


def _attn_kernel(sinks_ref, q_ref, kp_ref, kc_ref, vp_ref, vc_ref, o_ref):
    blk = pl.program_id(1)
    bq = q_ref.shape[0]
    kall = jnp.concatenate([kp_ref[...], kc_ref[...]], axis=0)
    vall = jnp.concatenate([vp_ref[...], vc_ref[...]], axis=0)
    row = lax.broadcasted_iota(jnp.int32, (2 * bq, 2 * bq), 0) & (bq - 1)
    col = lax.broadcasted_iota(jnp.int32, (2 * bq, 2 * bq), 1)
    valid2 = (col > row) & (col <= row + WINDOW) & ((col >= bq) | (blk > 0))
    top = lax.broadcasted_iota(jnp.int32, (2 * bq, 1), 0) < bq

    for h in range(N_KV_HEADS):
        c = h // 2
        k_lo, k_hi = _place_halves(kall[:, c * LANES:(c + 1) * LANES], h % 2)
        v_lo, v_hi = _place_halves(vall[:, c * LANES:(c + 1) * LANES], h % 2)
        qs = jnp.concatenate([q_ref[:, (2 * h) * LANES:(2 * h + 1) * LANES],
                              q_ref[:, (2 * h + 1) * LANES:(2 * h + 2) * LANES]], axis=0)
        acc = None
        for par, kmat, vmat in ((0, k_lo, v_lo), (1, k_hi, v_hi)):
            s = _dot_nt(qs, kmat.astype(BF16))
            s = jnp.where(valid2, s, NEG)
            sink = jnp.where(top, sinks_ref[4 * h + par], sinks_ref[4 * h + 2 + par])
            m = jnp.maximum(jnp.max(s, axis=-1, keepdims=True), sink)
            e = jnp.exp(s - m)
            den = jnp.sum(e, axis=-1, keepdims=True) + jnp.exp(sink - m)
            pv = _dot(e.astype(BF16), vmat.astype(BF16)) / den
            acc = pv if acc is None else acc + pv
        o_ref[:, (2 * h) * LANES:(2 * h + 1) * LANES] = acc[0:bq].astype(BF16)
        o_ref[:, (2 * h + 1) * LANES:(2 * h + 2) * LANES] = acc[bq:2 * bq].astype(BF16)


def _attention(q, k, v, sinks, batch, seq):
    bq = WINDOW
    nb = seq // bq
    cur = lambda b, i: (b * nb + i, 0)
    prev = lambda b, i: (b * nb + jnp.maximum(i - 1, 0), 0)
    return pl.pallas_call(
        _attn_kernel,
        grid=(batch, nb),
        in_specs=[
            pl.BlockSpec(memory_space=pltpu.SMEM),
            pl.BlockSpec((bq, Q_W), cur),
            pl.BlockSpec((bq, KV_W), prev),
            pl.BlockSpec((bq, KV_W), cur),
            pl.BlockSpec((bq, KV_W), prev),
            pl.BlockSpec((bq, KV_W), cur),
        ],
        out_specs=pl.BlockSpec((bq, Q_W), cur),
        out_shape=jax.ShapeDtypeStruct((batch * seq, Q_W), BF16),
        compiler_params=pltpu.CompilerParams(dimension_semantics=("arbitrary", "arbitrary")),
        name="swa_attention",
    )(sinks, q, k, k, v, v)


def _sample_attn_kernel(sinks_ref, qb_ref, knew_ref, vnew_ref, ck_ref, cv_ref,
                        nk_ref, nv_ref, o_ref):
    nb, wb, _ = ck_ref.shape
    sink = sinks_ref[...]
    for b in range(nb):
        nk_ref[b, 0:wb - 1, :] = ck_ref[b, 1:wb, :]
        nk_ref[b, wb - 1:wb, :] = knew_ref[b]
        nv_ref[b, 0:wb - 1, :] = cv_ref[b, 1:wb, :]
        nv_ref[b, wb - 1:wb, :] = vnew_ref[b]
        keys = nk_ref[b].astype(BF16)
        vals = nv_ref[b].astype(BF16)
        s = _dot_nt(qb_ref[b], keys)
        m = jnp.maximum(jnp.max(s, axis=-1, keepdims=True), sink)
        e = jnp.exp(s - m)
        den = jnp.sum(e, axis=-1, keepdims=True) + jnp.exp(sink - m)
        o_ref[b] = _dot(e.astype(BF16), vals) / den


def _sample_attention(qb, k_new, v_new, cache_k, cache_v, sinks_col, nb):
    bd, wb, _ = cache_k.shape
    blk3 = lambda i: (i, 0, 0)
    return pl.pallas_call(
        _sample_attn_kernel,
        grid=(bd // nb,),
        in_specs=[
            _const_spec((N_HEADS, 1)),
            pl.BlockSpec((nb, N_HEADS, KV_W), blk3),
            pl.BlockSpec((nb, 1, KV_W), blk3),
            pl.BlockSpec((nb, 1, KV_W), blk3),
            pl.BlockSpec((nb, wb, KV_W), blk3),
            pl.BlockSpec((nb, wb, KV_W), blk3),
        ],
        out_specs=[
            pl.BlockSpec((nb, wb, KV_W), blk3),
            pl.BlockSpec((nb, wb, KV_W), blk3),
            pl.BlockSpec((nb, N_HEADS, KV_W), blk3),
        ],
        out_shape=[
            jax.ShapeDtypeStruct((bd, wb, KV_W), F32),
            jax.ShapeDtypeStruct((bd, wb, KV_W), F32),
            jax.ShapeDtypeStruct((bd, N_HEADS, KV_W), F32),
        ],
        compiler_params=pltpu.CompilerParams(dimension_semantics=("arbitrary",)),
        name="sample_attention",
    )(sinks_col, qb, k_new, v_new, cache_k, cache_v)


def _sample_conv_kernel(a_ref, st_ref, dw_ref, dwb_ref, y_ref, ns_ref):
    nb, hist, _ = st_ref.shape
    w_hist = dw_ref[0:hist, :]
    w_last = dw_ref[hist:hist + 1, :]
    for b in range(nb):
        a_new = a_ref[b]
        y_ref[b] = (jnp.sum(st_ref[b] * w_hist, axis=0, keepdims=True)
                    + a_new * w_last + dwb_ref[...])
        ns_ref[b, 0:hist - 1, :] = st_ref[b, 1:hist, :]
        ns_ref[b, hist - 1:hist, :] = a_new


def _sample_conv(a_new, state, conv_dw, conv_dw_b, nb):
    bd, hist, c = state.shape
    blk3 = lambda i: (i, 0, 0)
    return pl.pallas_call(
        _sample_conv_kernel,
        grid=(bd // nb,),
        in_specs=[
            pl.BlockSpec((nb, 1, c), blk3),
            pl.BlockSpec((nb, hist, c), blk3),
            _const_spec((CONV_W, c)),
            _const_spec((1, c)),
        ],
        out_specs=[
            pl.BlockSpec((nb, 1, c), blk3),
            pl.BlockSpec((nb, hist, c), blk3),
        ],
        out_shape=[
            jax.ShapeDtypeStruct((bd, 1, c), F32),
            jax.ShapeDtypeStruct((bd, hist, c), F32),
        ],
        compiler_params=pltpu.CompilerParams(dimension_semantics=("arbitrary",)),
        name="sample_conv",
    )(a_new, state, conv_dw, conv_dw_b)


CONV_ROWS = 64


def _conv_tile(hist_ref, dw_ref, dwb_ref, y_ref, tm):
    lead = HALO - (CONV_W - 1)

    for r0 in range(0, tm, CONV_ROWS):
        for c0 in range(0, C_CONV, LANES):
            acc = jnp.broadcast_to(dwb_ref[:, c0:c0 + LANES], (CONV_ROWS, LANES))
            for j in range(CONV_W):
                acc = acc + dw_ref[j:j + 1, c0:c0 + LANES] * hist_ref[r0 + lead + j:r0 + lead + j + CONV_ROWS,
                                                                       c0:c0 + LANES]
            y_ref[r0:r0 + CONV_ROWS, c0:c0 + LANES] = acc


def _mix_post(y, x, attn, gates, lng, lnb, wco, bco, wo, wout):
    mu = jnp.mean(y, axis=-1, keepdims=True)
    d = y - mu
    var = jnp.mean(d * d, axis=-1, keepdims=True)
    yn = d * lax.rsqrt(var + EPS) * lng + lnb
    act = (yn * jax.nn.sigmoid(yn)).astype(BF16)
    conv_o = _dot(act, wco) + bco
    attn_p = _dot(attn, wo)
    mixed = (gates[:, 0:D_MODEL].astype(F32) * attn_p
             + gates[:, D_MODEL:2 * D_MODEL].astype(F32) * conv_o)
    return x + _dot(mixed.astype(BF16), wout)


def _mix_prompt_kernel(x_ref, a_ref, halo_ref, attn_ref, g_ref, dw_ref, dwb_ref, lng_ref, lnb_ref,
                       wco_ref, bco_ref, wo_ref, wout_ref, h_ref, hist_ref, y_ref):
    tm = x_ref.shape[0]
    first = pl.program_id(1) == 0
    hist_ref[0:HALO, :] = jnp.where(first, 0.0, halo_ref[...])
    hist_ref[HALO:HALO + tm, :] = a_ref[...]
    _conv_tile(hist_ref, dw_ref, dwb_ref, y_ref, tm)
    h_ref[...] = _mix_post(y_ref[...], x_ref[...], attn_ref[...], g_ref[...], lng_ref[...], lnb_ref[...],
                           wco_ref[...], bco_ref[...], wo_ref[...], wout_ref[...])


def _mix_sample_kernel(x_ref, y_ref, attn_ref, g_ref, lng_ref, lnb_ref,
                       wco_ref, bco_ref, wo_ref, wout_ref, h_ref):
    h_ref[...] = _mix_post(y_ref[...], x_ref[...], attn_ref[...], g_ref[...], lng_ref[...], lnb_ref[...],
                           wco_ref[...], bco_ref[...], wo_ref[...], wout_ref[...])


def _mix_prompt(x2d, a, attn, gates, batch, seq, tm, dw, dwb, lng, lnb, wco, bco, wo, wout):
    nt = seq // tm
    row = lambda b, i: (b * nt + i, 0)
    halo = lambda b, i: (b * (seq // HALO) + jnp.maximum(i * (tm // HALO) - 1, 0), 0)
    sq = (D_MODEL, D_MODEL)
    return pl.pallas_call(
        _mix_prompt_kernel,
        grid=(batch, nt),
        in_specs=[
            pl.BlockSpec((tm, D_MODEL), row),
            pl.BlockSpec((tm, C_CONV), row),
            pl.BlockSpec((HALO, C_CONV), halo),
            pl.BlockSpec((tm, Q_W), row),
            pl.BlockSpec((tm, 2 * D_MODEL), row),
            _const_spec((CONV_W, C_CONV)),
            _const_spec((1, C_CONV)),
            _const_spec((1, C_CONV)),
            _const_spec((1, C_CONV)),
            _const_spec(sq),
            _const_spec((1, D_MODEL)),
            _const_spec(sq),
            _const_spec(sq),
        ],
        out_specs=pl.BlockSpec((tm, D_MODEL), row),
        out_shape=jax.ShapeDtypeStruct((batch * seq, D_MODEL), F32),
        scratch_shapes=[pltpu.VMEM((HALO + tm, C_CONV), F32), pltpu.VMEM((tm, C_CONV), F32)],
        compiler_params=pltpu.CompilerParams(dimension_semantics=("arbitrary", "arbitrary"),
                                             vmem_limit_bytes=VMEM_LIMIT),
        name="mix_prompt",
    )(x2d, a, a, attn, gates, dw, dwb, lng, lnb, wco, bco, wo, wout)


def _mix_sample(x2d, y, attn, gates, lng, lnb, wco, bco, wo, wout):
    n = x2d.shape[0]
    sq = (D_MODEL, D_MODEL)
    full = lambda w: pl.BlockSpec((n, w), lambda i: (0, 0))
    return pl.pallas_call(
        _mix_sample_kernel,
        grid=(1,),
        in_specs=[full(D_MODEL), full(C_CONV), full(Q_W), full(2 * D_MODEL),
                  _const_spec((1, C_CONV)), _const_spec((1, C_CONV)),
                  _const_spec(sq), _const_spec((1, D_MODEL)), _const_spec(sq), _const_spec(sq)],
        out_specs=full(D_MODEL),
        out_shape=jax.ShapeDtypeStruct((n, D_MODEL), F32),
        compiler_params=pltpu.CompilerParams(dimension_semantics=("arbitrary",),
                                             vmem_limit_bytes=VMEM_LIMIT),
        name="mix_sample",
    )(x2d, y, attn, gates, lng, lnb, wco, bco, wo, wout)


FF_CHUNK = 1024


def _ffn_kernel(h_ref, p_ref, ln2_ref, w1_ref, w2_ref, lnp_ref, wg_ref, wp_ref, o_ref, u_ref, acc_ref):
    h = h_ref[...]
    u_ref[...] = (h * _rms_scale(h) * ln2_ref[...]).astype(BF16)
    acc_ref[...] = h
    for c0 in range(0, D_FF, FF_CHUNK):
        hid = jnp.maximum(_dot(u_ref[...], w1_ref[:, c0:c0 + FF_CHUNK]), 0.0)
        acc_ref[...] += _dot((hid * hid).astype(BF16), w2_ref[c0:c0 + FF_CHUNK, :])
    h2 = acc_ref[...]
    un = (h2 * _rms_scale(h2) * lnp_ref[...]).astype(BF16)
    gate = jax.nn.sigmoid(_dot(un, wg_ref[...]))
    o_ref[...] = h2 + gate * _dot(p_ref[...].astype(BF16), wp_ref[...])


def _ffn(h, p2d, tm, ln2, w1, w2, lnp, wg, wp):
    n = h.shape[0]
    d_ple = p2d.shape[1]
    row = lambda i: (i, 0)
    return pl.pallas_call(
        _ffn_kernel,
        grid=(n // tm,),
        in_specs=[
            pl.BlockSpec((tm, D_MODEL), row),
            pl.BlockSpec((tm, d_ple), row),
            _const_spec((1, D_MODEL)),
            _const_spec((D_MODEL, D_FF)),
            _const_spec((D_FF, D_MODEL)),
            _const_spec((1, D_MODEL)),
            _const_spec((D_MODEL, D_MODEL)),
            _const_spec((d_ple, D_MODEL)),
        ],
        out_specs=pl.BlockSpec((tm, D_MODEL), row),
        out_shape=jax.ShapeDtypeStruct((n, D_MODEL), F32),
        scratch_shapes=[pltpu.VMEM((tm, D_MODEL), BF16), pltpu.VMEM((tm, D_MODEL), F32)],
        compiler_params=pltpu.CompilerParams(dimension_semantics=("arbitrary",),
                                             vmem_limit_bytes=VMEM_LIMIT),
        name="ffn",
    )(h, p2d, ln2, w1, w2, lnp, wg, wp)


def _rope_tables(pos):
    half = HEAD_DIM // 2
    inv = jnp.power(jnp.float32(ROPE_THETA), -jnp.arange(half, dtype=F32) / half)
    ang = pos.astype(F32)[:, None] * inv[None, :]
    cos = jnp.tile(jnp.cos(ang), (1, LANES // half))
    sign = jnp.tile(jnp.concatenate([-jnp.ones((half,), F32), jnp.ones((half,), F32)]), LANES // HEAD_DIM)
    sin = jnp.tile(jnp.sin(ang), (1, LANES // half)) * sign[None, :]
    return cos, sin


def _row(v):
    return v.reshape(1, -1)


def kernel(x_prompt, x_sample, cache_k, cache_v, state_conv, p_prompt, p_sample, ln1, w_in, b_glu, q_norm, k_norm, sinks, w_o_attn, conv_dw, conv_dw_b, conv_ln_g, conv_ln_b, w_conv_out, b_conv_out, w_out, ln2, w_ff1, w_ff2, ln_ple, w_ple_gate, w_ple):
    depth = w_in.shape[0]
    batch, seq, _ = x_prompt.shape
    bd, dseq, _ = x_sample.shape
    wb = cache_k.shape[2]
    hist = state_conv.shape[2]
    past_len = PAST_LEN
    assert dseq == 1 and wb == WINDOW and hist == CONV_W - 1 and seq % WINDOW == 0

    tm = 512
    seg = (jnp.arange(MXU_N)[:, None] // HEAD_DIM == jnp.arange(MXU_N)[None, :] // HEAD_DIM).astype(BF16)
    cos_p, sin_p = _rope_tables(jnp.arange(seq, dtype=jnp.int32))
    cos_s, sin_s = _rope_tables(jnp.full((bd,), past_len, dtype=jnp.int32))
    head_ids = jnp.arange(N_HEADS)

    xp = x_prompt.reshape(batch * seq, D_MODEL)
    xs = x_sample.reshape(bd, D_MODEL)
    outs = [[] for _ in range(6)]
    for i in range(depth):
        win = w_in[i].astype(BF16)
        qg = _row(jnp.tile(q_norm[i], N_HEADS))
        kg = _row(jnp.tile(k_norm[i], N_KV_HEADS))
        wco = w_conv_out[i].astype(BF16)
        wo = w_o_attn[i].astype(BF16)
        wout = w_out[i].astype(BF16)
        w1 = w_ff1[i].astype(BF16)
        w2 = w_ff2[i].astype(BF16)
        wg = w_ple_gate[i].astype(BF16)
        wp = w_ple[i].astype(BF16)
        inproj_w = (_row(ln1[i]), win, _row(b_glu[i]), qg, kg, seg)
        post_w = (_row(conv_ln_g[i]), _row(conv_ln_b[i]), wco, _row(b_conv_out[i]), wo, wout)
        ffn_w = (_row(ln2[i]), w1, w2, _row(ln_ple[i]), wg, wp)

        q, k, v, a, g = _inproj(xp, cos_p, sin_p, seq // tm, tm, *inproj_w)
        attn = _attention(q, k, v, sinks[i], batch, seq)
        h = _mix_prompt(xp, a, attn, g, batch, seq, tm, conv_dw[i], _row(conv_dw_b[i]), *post_w)
        xp = _ffn(h, p_prompt[i].reshape(batch * seq, -1), tm, *ffn_w)
        outs[0].append(k.reshape(batch, seq, N_KV_HEADS, HEAD_DIM)[:, seq - WINDOW:])
        outs[1].append(v.reshape(batch, seq, N_KV_HEADS, HEAD_DIM)[:, seq - WINDOW:])
        outs[2].append(a.reshape(batch, seq, C_CONV)[:, seq - hist:])

        qs, ks, vs, a_s, gs = _inproj(xs, cos_s, sin_s, 1, bd, *inproj_w)
        qb = (jnp.zeros((bd, N_HEADS, N_KV_HEADS, HEAD_DIM), BF16)
              .at[:, head_ids, head_ids // GROUP].set(qs.reshape(bd, N_HEADS, HEAD_DIM))
              .reshape(bd, N_HEADS, KV_W))
        nk, nv, ao = _sample_attention(qb, ks.reshape(bd, 1, KV_W), vs.reshape(bd, 1, KV_W),
                                       cache_k[i].reshape(bd, wb, KV_W), cache_v[i].reshape(bd, wb, KV_W),
                                       sinks[i].reshape(N_HEADS, 1), 8)
        attn_s = (ao.reshape(bd, N_HEADS, N_KV_HEADS, HEAD_DIM)[:, head_ids, head_ids // GROUP]
                  .reshape(bd, Q_W).astype(BF16))
        y_s, ns = _sample_conv(a_s.reshape(bd, 1, C_CONV), state_conv[i], conv_dw[i], _row(conv_dw_b[i]), 8)
        hs = _mix_sample(xs, y_s.reshape(bd, C_CONV), attn_s, gs, *post_w)
        xs = _ffn(hs, p_sample[i].reshape(bd, -1), bd, *ffn_w)
        outs[3].append(nk.reshape(bd, wb, N_KV_HEADS, HEAD_DIM))
        outs[4].append(nv.reshape(bd, wb, N_KV_HEADS, HEAD_DIM))
        outs[5].append(ns)

    return (xp.reshape(batch, seq, D_MODEL), xs.reshape(bd, dseq, D_MODEL),
            jnp.stack(outs[0]), jnp.stack(outs[1]), jnp.stack(outs[2]),
            jnp.stack(outs[3]), jnp.stack(outs[4]), jnp.stack(outs[5]))
```

```python
import functools

import jax
import jax.numpy as jnp
from jax import lax
from jax.experimental import pallas as pl
from jax.experimental.pallas import tpu as pltpu

D_MODEL = 1024
N_HEADS = 16
N_KV_HEADS = 4
HEAD_DIM = 64
GROUP = N_HEADS // N_KV_HEADS
WINDOW = 128
ROPE_THETA = 10000.0
PAST_LEN = 16384
C_CONV = D_MODEL
CONV_W = 31
D_FF = 4 * D_MODEL
EPS = 1e-6
NEG = -1e30
Q_W = N_HEADS * HEAD_DIM
KV_W = N_KV_HEADS * HEAD_DIM
D_IN = Q_W + 2 * KV_W + 2 * C_CONV + 2 * D_MODEL

LOG2E = 1.4426950408889634
Q_SCALE = HEAD_DIM ** -0.5 * LOG2E

LANES = 128
SUBLANES = 8
BF16_ROWS = 16
MXU_N = 256
HALO = 32
VMEM_LIMIT = 56 * 1024 * 1024

F32 = jnp.float32
BF16 = jnp.bfloat16


def _dot(a, b):
    return jnp.dot(a, b, preferred_element_type=F32)


def _dot_nt(a, b):
    return lax.dot_general(a, b, (((1,), (1,)), ((), ())), preferred_element_type=F32)


def _const_spec(shape):
    nd = len(shape)
    return pl.BlockSpec(shape, lambda *_: (0,) * nd, pipeline_mode=pl.Buffered(1))


def _rms_scale(x):
    return lax.rsqrt(jnp.mean(x * x, axis=-1, keepdims=True) + EPS)


def _head_norm_rope(z, gain, seg, cos, sin_signed, scale):
    tm, width = z.shape
    outs = []
    lane = lax.broadcasted_iota(jnp.int32, (tm, LANES), 1)
    first_half = (lane & (HEAD_DIM - 1)) < (HEAD_DIM // 2)
    for c0 in range(0, width, MXU_N):
        zc = z[:, c0:c0 + MXU_N]
        ss = _dot((zc * zc).astype(BF16), seg)
        zn = zc * lax.rsqrt(ss * (1.0 / HEAD_DIM) + EPS) * gain[:, c0:c0 + MXU_N]
        for l0 in range(0, MXU_N, LANES):
            xc = zn[:, l0:l0 + LANES]
            partner = jnp.where(first_half,
                                pltpu.roll(xc, LANES - HEAD_DIM // 2, axis=1),
                                pltpu.roll(xc, HEAD_DIM // 2, axis=1))
            outs.append((xc * cos + partner * sin_signed) * scale)
    return outs


def _store_head_pairs(dup_ref, i, xc):
    low = lax.broadcasted_iota(jnp.int32, xc.shape, 1) < HEAD_DIM
    rolled = pltpu.roll(xc, HEAD_DIM, axis=1)
    dup_ref[:, (2 * i) * LANES:(2 * i + 1) * LANES] = jnp.where(low, xc, rolled).astype(BF16)
    dup_ref[:, (2 * i + 1) * LANES:(2 * i + 2) * LANES] = jnp.where(low, rolled, xc).astype(BF16)


def _inproj_kernel(x_ref, ln1_ref, w_ref, bglu_ref, qg_ref, kg_ref, cos_ref, sin_ref, seg_ref,
                   q_ref, k_ref, v_ref, kd_ref, vd_ref, a_ref, g_ref, u_ref):
    x = x_ref[...]
    u_ref[...] = (x * _rms_scale(x) * ln1_ref[...]).astype(BF16)
    cos = cos_ref[...]
    sin = sin_ref[...]
    seg = seg_ref[...]

    zq = _dot(u_ref[...], w_ref[:, 0:Q_W])
    for i, qc in enumerate(_head_norm_rope(zq, qg_ref[...], seg, cos, sin, Q_SCALE)):
        q_ref[:, i * LANES:(i + 1) * LANES] = qc.astype(BF16)

    zk = _dot(u_ref[...], w_ref[:, Q_W:Q_W + KV_W])
    for i, kc in enumerate(_head_norm_rope(zk, kg_ref[...], seg, cos, sin, 1.0)):
        k_ref[:, i * LANES:(i + 1) * LANES] = kc
        _store_head_pairs(kd_ref, i, kc)

    zv = _dot(u_ref[...], w_ref[:, Q_W + KV_W:Q_W + 2 * KV_W])
    v_ref[...] = zv
    for i in range(KV_W // LANES):
        _store_head_pairs(vd_ref, i, zv[:, i * LANES:(i + 1) * LANES])

    glu0 = Q_W + 2 * KV_W
    for c0 in range(0, C_CONV, MXU_N):
        lin = _dot(u_ref[...],w_ref[:, glu0 + c0:glu0 + c0 + MXU_N]) + bglu_ref[:, c0:c0 + MXU_N]
        gate = (_dot(u_ref[...],w_ref[:, glu0 + C_CONV + c0:glu0 + C_CONV + c0 + MXU_N])
                + bglu_ref[:, C_CONV + c0:C_CONV + c0 + MXU_N])
        a_ref[:, c0:c0 + MXU_N] = lin * jax.nn.sigmoid(gate)

    g0 = glu0 + 2 * C_CONV
    for c0 in range(0, 2 * D_MODEL, MXU_N):
        g_ref[:, c0:c0 + MXU_N] = jax.nn.sigmoid(_dot(u_ref[...],w_ref[:, g0 + c0:g0 + c0 + MXU_N])).astype(BF16)


def _inproj(x2d, cos, sin, pos_tiles, tm, ln1, w_in, b_glu, qg, kg, seg):
    n = x2d.shape[0]
    grid = (n // tm,)
    row = lambda i: (i, 0)
    pos = lambda i: (i % pos_tiles, 0)
    return pl.pallas_call(
        _inproj_kernel,
        grid=grid,
        in_specs=[
            pl.BlockSpec((tm, D_MODEL), row),
            _const_spec((1, D_MODEL)),
            _const_spec((D_MODEL, D_IN)),
            _const_spec((1, 2 * C_CONV)),
            _const_spec((1, Q_W)),
            _const_spec((1, KV_W)),
            pl.BlockSpec((tm, LANES), pos),
            pl.BlockSpec((tm, LANES), pos),
            _const_spec((MXU_N, MXU_N)),
        ],
        out_specs=[
            pl.BlockSpec((tm, Q_W), row),
            pl.BlockSpec((tm, KV_W), row),
            pl.BlockSpec((tm, KV_W), row),
            pl.BlockSpec((tm, 2 * KV_W), row),
            pl.BlockSpec((tm, 2 * KV_W), row),
            pl.BlockSpec((tm, C_CONV), row),
            pl.BlockSpec((tm, 2 * D_MODEL), row),
        ],
        out_shape=[
            jax.ShapeDtypeStruct((n, Q_W), BF16),
            jax.ShapeDtypeStruct((n, KV_W), F32),
            jax.ShapeDtypeStruct((n, KV_W), F32),
            jax.ShapeDtypeStruct((n, 2 * KV_W), BF16),
            jax.ShapeDtypeStruct((n, 2 * KV_W), BF16),
            jax.ShapeDtypeStruct((n, C_CONV), F32),
            jax.ShapeDtypeStruct((n, 2 * D_MODEL), BF16),
        ],
        scratch_shapes=[pltpu.VMEM((tm, D_MODEL), BF16)],
        compiler_params=pltpu.CompilerParams(dimension_semantics=("arbitrary",),
                                             vmem_limit_bytes=VMEM_LIMIT),
        name="in_proj",
    )(x2d, ln1, w_in, b_glu, qg, kg, cos, sin, seg)


def _attn_kernel(sinks_ref, q_ref, kp_ref, kc_ref, vp_ref, vc_ref, o_ref):
    blk = pl.program_id(1)
    bq = q_ref.shape[0]
    nk = 2 * bq
    qrow = lax.broadcasted_iota(jnp.int32, (bq, nk), 0)
    col = lax.broadcasted_iota(jnp.int32, (bq, nk), 1)
    valid = (col > qrow) & (col <= qrow + WINDOW) & ((col >= bq) | (blk > 0))
    base = jnp.where(valid, 0.0, NEG)
    lane = lax.broadcasted_iota(jnp.int32, (bq, LANES), 1)
    sink_slot = lane == 0
    low = lane < HEAD_DIM
    first_row = lax.broadcasted_iota(jnp.int32, (BF16_ROWS, LANES), 0) == 0
    ones = jnp.ones((nk, LANES), BF16)

    def window(prev_ref, cur_ref, lanes):
        head = prev_ref[0:BF16_ROWS, lanes]
        head = jnp.where(first_row, jnp.zeros_like(head), head)
        return jnp.concatenate([head, prev_ref[BF16_ROWS:, lanes], cur_ref[:, lanes]], axis=0)

    for h in range(N_KV_HEADS):
        lanes = slice(h * LANES, (h + 1) * LANES)
        kmat = window(kp_ref, kc_ref, lanes)
        vmat = window(vp_ref, vc_ref, lanes)
        vrhs = jnp.concatenate([vmat, ones], axis=1)
        q_parts = []
        bias_parts = []
        for g in range(GROUP):
            qc = q_ref[:, (2 * h + g // 2) * LANES:(2 * h + g // 2 + 1) * LANES]
            zero = jnp.zeros_like(qc)
            q_parts.append(jnp.where(low, qc, zero) if g % 2 == 0 else jnp.where(low, zero, qc))
            sink = sinks_ref[GROUP * h + g] * LOG2E
            bias_parts.append(jnp.concatenate(
                [jnp.where(sink_slot, sink, base[:, 0:LANES]), base[:, LANES:]], axis=1))
        s = _dot_nt(jnp.concatenate(q_parts, axis=0), kmat) + jnp.concatenate(bias_parts, axis=0)
        m = jnp.max(s, axis=-1, keepdims=True)
        e = jnp.exp2(s - m).astype(BF16)
        pvd = _dot(e, vrhs)
        o = pvd[:, 0:LANES] / pvd[:, LANES:2 * LANES]
        for half in range(2):
            r0 = 2 * half * bq
            o_ref[:, (2 * h + half) * LANES:(2 * h + half + 1) * LANES] = jnp.where(
                low, o[r0:r0 + bq], o[r0 + bq:r0 + 2 * bq]).astype(BF16)


def _attention(q, kd, vd, sinks, batch, seq):
    bq = WINDOW
    nb = seq // bq
    kvw = N_KV_HEADS * LANES
    cur = lambda b, i: (b * nb + i, 0)
    prev = lambda b, i: (b * nb + jnp.maximum(i - 1, 0), 0)
    return pl.pallas_call(
        _attn_kernel,
        grid=(batch, nb),
        in_specs=[
            pl.BlockSpec(memory_space=pltpu.SMEM),
            pl.BlockSpec((bq, Q_W), cur),
            pl.BlockSpec((bq, kvw), prev),
            pl.BlockSpec((bq, kvw), cur),
            pl.BlockSpec((bq, kvw), prev),
            pl.BlockSpec((bq, kvw), cur),
        ],
        out_specs=pl.BlockSpec((bq, Q_W), cur),
        out_shape=jax.ShapeDtypeStruct((batch * seq, Q_W), BF16),
        compiler_params=pltpu.CompilerParams(dimension_semantics=("arbitrary", "arbitrary")),
        name="swa_attention",
    )(sinks, q, kd, kd, vd, vd)


def _sample_attn_kernel(sinks_ref, qb_ref, knew_ref, vnew_ref, ck_ref, cv_ref,
                        nk_ref, nv_ref, o_ref):
    nb, wb, _ = ck_ref.shape
    sink = sinks_ref[...] * LOG2E
    for b in range(nb):
        nk_ref[b, 0:wb - 1, :] = ck_ref[b, 1:wb, :]
        nk_ref[b, wb - 1:wb, :] = knew_ref[b]
        nv_ref[b, 0:wb - 1, :] = cv_ref[b, 1:wb, :]
        nv_ref[b, wb - 1:wb, :] = vnew_ref[b]
        keys = nk_ref[b].astype(BF16)
        vals = nv_ref[b].astype(BF16)
        s = _dot_nt(qb_ref[b], keys)
        m = jnp.maximum(jnp.max(s, axis=-1, keepdims=True), sink)
        e = jnp.exp2(s - m)
        den = jnp.sum(e, axis=-1, keepdims=True) + jnp.exp2(sink - m)
        o_ref[b] = _dot(e.astype(BF16), vals) / den


def _sample_attention(qb, k_new, v_new, cache_k, cache_v, sinks_col, nb):
    bd, wb, _ = cache_k.shape
    blk3 = lambda i: (i, 0, 0)
    return pl.pallas_call(
        _sample_attn_kernel,
        grid=(bd // nb,),
        in_specs=[
            _const_spec((N_HEADS, 1)),
            pl.BlockSpec((nb, N_HEADS, KV_W), blk3),
            pl.BlockSpec((nb, 1, KV_W), blk3),
            pl.BlockSpec((nb, 1, KV_W), blk3),
            pl.BlockSpec((nb, wb, KV_W), blk3),
            pl.BlockSpec((nb, wb, KV_W), blk3),
        ],
        out_specs=[
            pl.BlockSpec((nb, wb, KV_W), blk3),
            pl.BlockSpec((nb, wb, KV_W), blk3),
            pl.BlockSpec((nb, N_HEADS, KV_W), blk3),
        ],
        out_shape=[
            jax.ShapeDtypeStruct((bd, wb, KV_W), F32),
            jax.ShapeDtypeStruct((bd, wb, KV_W), F32),
            jax.ShapeDtypeStruct((bd, N_HEADS, KV_W), F32),
        ],
        compiler_params=pltpu.CompilerParams(dimension_semantics=("arbitrary",)),
        name="sample_attention",
    )(sinks_col, qb, k_new, v_new, cache_k, cache_v)


def _sample_conv_kernel(a_ref, st_ref, dw_ref, dwb_ref, y_ref, ns_ref):
    nb, hist, _ = st_ref.shape
    w_hist = dw_ref[0:hist, :]
    w_last = dw_ref[hist:hist + 1, :]
    for b in range(nb):
        a_new = a_ref[b]
        y_ref[b] = (jnp.sum(st_ref[b] * w_hist, axis=0, keepdims=True)
                    + a_new * w_last + dwb_ref[...])
        ns_ref[b, 0:hist - 1, :] = st_ref[b, 1:hist, :]
        ns_ref[b, hist - 1:hist, :] = a_new


def _sample_conv(a_new, state, conv_dw, conv_dw_b, nb):
    bd, hist, c = state.shape
    blk3 = lambda i: (i, 0, 0)
    return pl.pallas_call(
        _sample_conv_kernel,
        grid=(bd // nb,),
        in_specs=[
            pl.BlockSpec((nb, 1, c), blk3),
            pl.BlockSpec((nb, hist, c), blk3),
            _const_spec((CONV_W, c)),
            _const_spec((1, c)),
        ],
        out_specs=[
            pl.BlockSpec((nb, 1, c), blk3),
            pl.BlockSpec((nb, hist, c), blk3),
        ],
        out_shape=[
            jax.ShapeDtypeStruct((bd, 1, c), F32),
            jax.ShapeDtypeStruct((bd, hist, c), F32),
        ],
        compiler_params=pltpu.CompilerParams(dimension_semantics=("arbitrary",)),
        name="sample_conv",
    )(a_new, state, conv_dw, conv_dw_b)


CONV_ROWS = 64


N_CHUNKS = C_CONV // LANES
PITCH = N_CHUNKS + 1


def _hist_store(hist_ref, t0, rows, value):
    for c in range(N_CHUNKS):
        hist_ref[pl.ds(t0 * PITCH + c, rows, stride=PITCH), :] = value[:, c * LANES:(c + 1) * LANES]


def _conv_tile(hist_ref, dw_ref, dwb_ref, y_ref, tm):
    lead = HALO - (CONV_W - 1)

    groups = CONV_ROWS // SUBLANES

    def body(rb, carry):
        r0 = pl.multiple_of(rb * CONV_ROWS, CONV_ROWS)
        for c in range(N_CHUNKS):
            lanes = slice(c * LANES, (c + 1) * LANES)
            taps = [jnp.broadcast_to(dw_ref[j:j + 1, lanes], (SUBLANES, LANES)) for j in range(CONV_W)]
            accs = [jnp.broadcast_to(dwb_ref[:, lanes], (SUBLANES, LANES))] * groups
            for o in range(CONV_ROWS + CONV_W - SUBLANES):
                win = hist_ref[pl.ds((r0 + lead + o) * PITCH + c, SUBLANES, stride=PITCH), :]
                for k in range(groups):
                    j = o - SUBLANES * k
                    if 0 <= j < CONV_W:
                        accs[k] = accs[k] + taps[j] * win
            for k in range(groups):
                y_ref[pl.ds(pl.multiple_of(r0 + SUBLANES * k, SUBLANES), SUBLANES), lanes] = accs[k]
        return carry

    lax.fori_loop(0, tm // CONV_ROWS, body, 0)


def _mix_post(y, x, attn, gates, lng, lnb, wco, bco, wo, wout):
    mu = jnp.mean(y, axis=-1, keepdims=True)
    d = y - mu
    var = jnp.mean(d * d, axis=-1, keepdims=True)
    yn = d * lax.rsqrt(var + EPS) * lng + lnb
    act = (yn * jax.nn.sigmoid(yn)).astype(BF16)
    conv_o = _dot(act, wco) + bco
    attn_p = _dot(attn, wo)
    mixed = (gates[:, 0:D_MODEL].astype(F32) * attn_p
             + gates[:, D_MODEL:2 * D_MODEL].astype(F32) * conv_o)
    return x + _dot(mixed.astype(BF16), wout)


def _mix_prompt_kernel(x_ref, a_ref, halo_ref, attn_ref, g_ref, dw_ref, dwb_ref, lng_ref, lnb_ref,
                       wco_ref, bco_ref, wo_ref, wout_ref, h_ref, hist_ref, y_ref):
    tm = x_ref.shape[0]
    first = pl.program_id(1) == 0
    _hist_store(hist_ref, 0, HALO, jnp.where(first, 0.0, halo_ref[...]))
    _hist_store(hist_ref, HALO, tm, a_ref[...])
    _conv_tile(hist_ref, dw_ref, dwb_ref, y_ref, tm)
    h_ref[...] = _mix_post(y_ref[...], x_ref[...], attn_ref[...], g_ref[...], lng_ref[...], lnb_ref[...],
                           wco_ref[...], bco_ref[...], wo_ref[...], wout_ref[...])


def _mix_sample_kernel(x_ref, y_ref, attn_ref, g_ref, lng_ref, lnb_ref,
                       wco_ref, bco_ref, wo_ref, wout_ref, h_ref):
    h_ref[...] = _mix_post(y_ref[...], x_ref[...], attn_ref[...], g_ref[...], lng_ref[...], lnb_ref[...],
                           wco_ref[...], bco_ref[...], wo_ref[...], wout_ref[...])


def _mix_prompt(x2d, a, attn, gates, batch, seq, tm, dw, dwb, lng, lnb, wco, bco, wo, wout):
    nt = seq // tm
    row = lambda b, i: (b * nt + i, 0)
    halo = lambda b, i: (b * (seq // HALO) + jnp.maximum(i * (tm // HALO) - 1, 0), 0)
    sq = (D_MODEL, D_MODEL)
    return pl.pallas_call(
        _mix_prompt_kernel,
        grid=(batch, nt),
        in_specs=[
            pl.BlockSpec((tm, D_MODEL), row),
            pl.BlockSpec((tm, C_CONV), row),
            pl.BlockSpec((HALO, C_CONV), halo),
            pl.BlockSpec((tm, Q_W), row),
            pl.BlockSpec((tm, 2 * D_MODEL), row),
            _const_spec((CONV_W, C_CONV)),
            _const_spec((1, C_CONV)),
            _const_spec((1, C_CONV)),
            _const_spec((1, C_CONV)),
            _const_spec(sq),
            _const_spec((1, D_MODEL)),
            _const_spec(sq),
            _const_spec(sq),
        ],
        out_specs=pl.BlockSpec((tm, D_MODEL), row),
        out_shape=jax.ShapeDtypeStruct((batch * seq, D_MODEL), F32),
        scratch_shapes=[pltpu.VMEM(((HALO + tm) * PITCH, LANES), F32), pltpu.VMEM((tm, C_CONV), F32)],
        compiler_params=pltpu.CompilerParams(dimension_semantics=("arbitrary", "arbitrary"),
                                             vmem_limit_bytes=VMEM_LIMIT),
        name="mix_prompt",
    )(x2d, a, a, attn, gates, dw, dwb, lng, lnb, wco, bco, wo, wout)


def _mix_sample(x2d, y, attn, gates, lng, lnb, wco, bco, wo, wout):
    n = x2d.shape[0]
    sq = (D_MODEL, D_MODEL)
    full = lambda w: pl.BlockSpec((n, w), lambda i: (0, 0))
    return pl.pallas_call(
        _mix_sample_kernel,
        grid=(1,),
        in_specs=[full(D_MODEL), full(C_CONV), full(Q_W), full(2 * D_MODEL),
                  _const_spec((1, C_CONV)), _const_spec((1, C_CONV)),
                  _const_spec(sq), _const_spec((1, D_MODEL)), _const_spec(sq), _const_spec(sq)],
        out_specs=full(D_MODEL),
        out_shape=jax.ShapeDtypeStruct((n, D_MODEL), F32),
        compiler_params=pltpu.CompilerParams(dimension_semantics=("arbitrary",),
                                             vmem_limit_bytes=VMEM_LIMIT),
        name="mix_sample",
    )(x2d, y, attn, gates, lng, lnb, wco, bco, wo, wout)


FF_CHUNK = 1024


def _ffn_kernel(h_ref, p_ref, ln2_ref, w1_ref, w2_ref, lnp_ref, wg_ref, wp_ref, o_ref, u_ref, acc_ref):
    h = h_ref[...]
    u_ref[...] = (h * _rms_scale(h) * ln2_ref[...]).astype(BF16)
    acc_ref[...] = h
    for c0 in range(0, D_FF, FF_CHUNK):
        hid = jnp.maximum(_dot(u_ref[...], w1_ref[:, c0:c0 + FF_CHUNK]), 0.0)
        acc_ref[...] += _dot((hid * hid).astype(BF16), w2_ref[c0:c0 + FF_CHUNK, :])
    h2 = acc_ref[...]
    un = (h2 * _rms_scale(h2) * lnp_ref[...]).astype(BF16)
    gate = jax.nn.sigmoid(_dot(un, wg_ref[...]))
    o_ref[...] = h2 + gate * _dot(p_ref[...].astype(BF16), wp_ref[...])


def _ffn(h, p2d, tm, ln2, w1, w2, lnp, wg, wp):
    n = h.shape[0]
    d_ple = p2d.shape[1]
    row = lambda i: (i, 0)
    return pl.pallas_call(
        _ffn_kernel,
        grid=(n // tm,),
        in_specs=[
            pl.BlockSpec((tm, D_MODEL), row),
            pl.BlockSpec((tm, d_ple), row),
            _const_spec((1, D_MODEL)),
            _const_spec((D_MODEL, D_FF)),
            _const_spec((D_FF, D_MODEL)),
            _const_spec((1, D_MODEL)),
            _const_spec((D_MODEL, D_MODEL)),
            _const_spec((d_ple, D_MODEL)),
        ],
        out_specs=pl.BlockSpec((tm, D_MODEL), row),
        out_shape=jax.ShapeDtypeStruct((n, D_MODEL), F32),
        scratch_shapes=[pltpu.VMEM((tm, D_MODEL), BF16), pltpu.VMEM((tm, D_MODEL), F32)],
        compiler_params=pltpu.CompilerParams(dimension_semantics=("arbitrary",),
                                             vmem_limit_bytes=VMEM_LIMIT),
        name="ffn",
    )(h, p2d, ln2, w1, w2, lnp, wg, wp)


def _rope_tables(pos):
    half = HEAD_DIM // 2
    inv = jnp.power(jnp.float32(ROPE_THETA), -jnp.arange(half, dtype=F32) / half)
    ang = pos.astype(F32)[:, None] * inv[None, :]
    cos = jnp.tile(jnp.cos(ang), (1, LANES // half))
    sign = jnp.tile(jnp.concatenate([-jnp.ones((half,), F32), jnp.ones((half,), F32)]), LANES // HEAD_DIM)
    sin = jnp.tile(jnp.sin(ang), (1, LANES // half)) * sign[None, :]
    return cos, sin


def _row(v):
    return v.reshape(1, -1)


def kernel(x_prompt, x_sample, cache_k, cache_v, state_conv, p_prompt, p_sample, ln1, w_in, b_glu, q_norm, k_norm, sinks, w_o_attn, conv_dw, conv_dw_b, conv_ln_g, conv_ln_b, w_conv_out, b_conv_out, w_out, ln2, w_ff1, w_ff2, ln_ple, w_ple_gate, w_ple):
    depth = w_in.shape[0]
    batch, seq, _ = x_prompt.shape
    bd, dseq, _ = x_sample.shape
    wb = cache_k.shape[2]
    hist = state_conv.shape[2]
    past_len = PAST_LEN
    assert dseq == 1 and wb == WINDOW and hist == CONV_W - 1 and seq % WINDOW == 0

    tm = 512
    seg = (jnp.arange(MXU_N)[:, None] // HEAD_DIM == jnp.arange(MXU_N)[None, :] // HEAD_DIM).astype(BF16)
    cos_p, sin_p = _rope_tables(jnp.arange(seq, dtype=jnp.int32))
    cos_s, sin_s = _rope_tables(jnp.full((bd,), past_len, dtype=jnp.int32))
    head_ids = jnp.arange(N_HEADS)

    xp = x_prompt.reshape(batch * seq, D_MODEL)
    xs = x_sample.reshape(bd, D_MODEL)
    outs = [[] for _ in range(6)]
    for i in range(depth):
        win = w_in[i].astype(BF16)
        qg = _row(jnp.tile(q_norm[i], N_HEADS))
        kg = _row(jnp.tile(k_norm[i], N_KV_HEADS))
        wco = w_conv_out[i].astype(BF16)
        wo = w_o_attn[i].astype(BF16)
        wout = w_out[i].astype(BF16)
        w1 = w_ff1[i].astype(BF16)
        w2 = w_ff2[i].astype(BF16)
        wg = w_ple_gate[i].astype(BF16)
        wp = w_ple[i].astype(BF16)
        inproj_w = (_row(ln1[i]), win, _row(b_glu[i]), qg, kg, seg)
        post_w = (_row(conv_ln_g[i]), _row(conv_ln_b[i]), wco, _row(b_conv_out[i]), wo, wout)
        ffn_w = (_row(ln2[i]), w1, w2, _row(ln_ple[i]), wg, wp)

        q, k, v, kd, vd, a, g = _inproj(xp, cos_p, sin_p, seq // tm, tm, *inproj_w)
        attn = _attention(q, kd, vd, sinks[i], batch, seq)
        h = _mix_prompt(xp, a, attn, g, batch, seq, tm, conv_dw[i], _row(conv_dw_b[i]), *post_w)
        xp = _ffn(h, p_prompt[i].reshape(batch * seq, -1), tm, *ffn_w)
        outs[0].append(k.reshape(batch, seq, N_KV_HEADS, HEAD_DIM)[:, seq - WINDOW:])
        outs[1].append(v.reshape(batch, seq, N_KV_HEADS, HEAD_DIM)[:, seq - WINDOW:])
        outs[2].append(a.reshape(batch, seq, C_CONV)[:, seq - hist:])

        qs, ks, vs, _, _, a_s, gs = _inproj(xs, cos_s, sin_s, 1, bd, *inproj_w)
        qb = (jnp.zeros((bd, N_HEADS, N_KV_HEADS, HEAD_DIM), BF16)
              .at[:, head_ids, head_ids // GROUP].set(qs.reshape(bd, N_HEADS, HEAD_DIM))
              .reshape(bd, N_HEADS, KV_W))
        nk, nv, ao = _sample_attention(qb, ks.reshape(bd, 1, KV_W), vs.reshape(bd, 1, KV_W),
                                       cache_k[i].reshape(bd, wb, KV_W), cache_v[i].reshape(bd, wb, KV_W),
                                       sinks[i].reshape(N_HEADS, 1), 8)
        attn_s = (ao.reshape(bd, N_HEADS, N_KV_HEADS, HEAD_DIM)[:, head_ids, head_ids // GROUP]
                  .reshape(bd, Q_W).astype(BF16))
        y_s, ns = _sample_conv(a_s.reshape(bd, 1, C_CONV), state_conv[i], conv_dw[i], _row(conv_dw_b[i]), 8)
        hs = _mix_sample(xs, y_s.reshape(bd, C_CONV), attn_s, gs, *post_w)
        xs = _ffn(hs, p_sample[i].reshape(bd, -1), bd, *ffn_w)
        outs[3].append(nk.reshape(bd, wb, N_KV_HEADS, HEAD_DIM))
        outs[4].append(nv.reshape(bd, wb, N_KV_HEADS, HEAD_DIM))
        outs[5].append(ns)

    return (xp.reshape(batch, seq, D_MODEL), xs.reshape(bd, dseq, D_MODEL),
            jnp.stack(outs[0]), jnp.stack(outs[1]), jnp.stack(outs[2]),
            jnp.stack(outs[3]), jnp.stack(outs[4]), jnp.stack(outs[5]))
```

```python
import functools

import jax
import jax.numpy as jnp
from jax import lax
from jax.experimental import pallas as pl
from jax.experimental.pallas import tpu as pltpu

D_MODEL = 1024
N_HEADS = 16
N_KV_HEADS = 4
HEAD_DIM = 64
GROUP = N_HEADS // N_KV_HEADS
WINDOW = 128
ROPE_THETA = 10000.0
PAST_LEN = 16384
C_CONV = D_MODEL
CONV_W = 31
D_FF = 4 * D_MODEL
EPS = 1e-6
NEG = -1e30
Q_W = N_HEADS * HEAD_DIM
KV_W = N_KV_HEADS * HEAD_DIM
D_IN = Q_W + 2 * KV_W + 2 * C_CONV + 2 * D_MODEL

LOG2E = 1.4426950408889634
Q_SCALE = HEAD_DIM ** -0.5 * LOG2E

LANES = 128
SUBLANES = 8
BF16_ROWS = 16
MXU_N = 256
HALO = 32
VMEM_LIMIT = 56 * 1024 * 1024

F32 = jnp.float32
BF16 = jnp.bfloat16


def _dot(a, b):
    return jnp.dot(a, b, preferred_element_type=F32)


def _dot_nt(a, b):
    return lax.dot_general(a, b, (((1,), (1,)), ((), ())), preferred_element_type=F32)


def _const_spec(shape):
    nd = len(shape)
    return pl.BlockSpec(shape, lambda *_: (0,) * nd, pipeline_mode=pl.Buffered(1))


def _rms_scale(x):
    return lax.rsqrt(jnp.mean(x * x, axis=-1, keepdims=True) + EPS)


def _head_norm_rope(z, gain, seg, cos, sin_signed, scale):
    tm, width = z.shape
    outs = []
    lane = lax.broadcasted_iota(jnp.int32, (tm, LANES), 1)
    first_half = (lane & (HEAD_DIM - 1)) < (HEAD_DIM // 2)
    for c0 in range(0, width, MXU_N):
        zc = z[:, c0:c0 + MXU_N]
        ss = _dot((zc * zc).astype(BF16), seg)
        zn = zc * lax.rsqrt(ss * (1.0 / HEAD_DIM) + EPS) * gain[:, c0:c0 + MXU_N]
        for l0 in range(0, MXU_N, LANES):
            xc = zn[:, l0:l0 + LANES]
            partner = jnp.where(first_half,
                                pltpu.roll(xc, LANES - HEAD_DIM // 2, axis=1),
                                pltpu.roll(xc, HEAD_DIM // 2, axis=1))
            outs.append((xc * cos + partner * sin_signed) * scale)
    return outs


def _store_head_pairs(dup_ref, i, xc):
    low = lax.broadcasted_iota(jnp.int32, xc.shape, 1) < HEAD_DIM
    rolled = pltpu.roll(xc, HEAD_DIM, axis=1)
    dup_ref[:, (2 * i) * LANES:(2 * i + 1) * LANES] = jnp.where(low, xc, rolled).astype(BF16)
    dup_ref[:, (2 * i + 1) * LANES:(2 * i + 2) * LANES] = jnp.where(low, rolled, xc).astype(BF16)


def _inproj_kernel(x_ref, ln1_ref, w_ref, bglu_ref, qg_ref, kg_ref, cos_ref, sin_ref, seg_ref,
                   q_ref, k_ref, v_ref, kd_ref, vd_ref, a_ref, g_ref, u_ref):
    x = x_ref[...]
    u_ref[...] = (x * _rms_scale(x) * ln1_ref[...]).astype(BF16)
    cos = cos_ref[...]
    sin = sin_ref[...]
    seg = seg_ref[...]

    zq = _dot(u_ref[...], w_ref[:, 0:Q_W])
    for i, qc in enumerate(_head_norm_rope(zq, qg_ref[...], seg, cos, sin, Q_SCALE)):
        q_ref[:, i * LANES:(i + 1) * LANES] = qc.astype(BF16)

    zk = _dot(u_ref[...], w_ref[:, Q_W:Q_W + KV_W])
    for i, kc in enumerate(_head_norm_rope(zk, kg_ref[...], seg, cos, sin, 1.0)):
        k_ref[:, i * LANES:(i + 1) * LANES] = kc
        _store_head_pairs(kd_ref, i, kc)

    zv = _dot(u_ref[...], w_ref[:, Q_W + KV_W:Q_W + 2 * KV_W])
    v_ref[...] = zv
    for i in range(KV_W // LANES):
        _store_head_pairs(vd_ref, i, zv[:, i * LANES:(i + 1) * LANES])

    glu0 = Q_W + 2 * KV_W
    for c0 in range(0, C_CONV, MXU_N):
        lin = _dot(u_ref[...], w_ref[:, glu0 + c0:glu0 + c0 + MXU_N]) + bglu_ref[:, c0:c0 + MXU_N]
        gate = (_dot(u_ref[...], w_ref[:, glu0 + C_CONV + c0:glu0 + C_CONV + c0 + MXU_N])
                + bglu_ref[:, C_CONV + c0:C_CONV + c0 + MXU_N])
        a_ref[:, c0:c0 + MXU_N] = lin * jax.nn.sigmoid(gate)

    g0 = glu0 + 2 * C_CONV
    for c0 in range(0, 2 * D_MODEL, MXU_N):
        g_ref[:, c0:c0 + MXU_N] = jax.nn.sigmoid(_dot(u_ref[...], w_ref[:, g0 + c0:g0 + c0 + MXU_N])).astype(BF16)


def _inproj(x2d, cos, sin, pos_tiles, tm, ln1, w_in, b_glu, qg, kg, seg):
    n = x2d.shape[0]
    grid = (n // tm,)
    row = lambda i: (i, 0)
    pos = lambda i: (i % pos_tiles, 0)
    return pl.pallas_call(
        _inproj_kernel,
        grid=grid,
        in_specs=[
            pl.BlockSpec((tm, D_MODEL), row),
            _const_spec((1, D_MODEL)),
            _const_spec((D_MODEL, D_IN)),
            _const_spec((1, 2 * C_CONV)),
            _const_spec((1, Q_W)),
            _const_spec((1, KV_W)),
            pl.BlockSpec((tm, LANES), pos),
            pl.BlockSpec((tm, LANES), pos),
            _const_spec((MXU_N, MXU_N)),
        ],
        out_specs=[
            pl.BlockSpec((tm, Q_W), row),
            pl.BlockSpec((tm, KV_W), row),
            pl.BlockSpec((tm, KV_W), row),
            pl.BlockSpec((tm, 2 * KV_W), row),
            pl.BlockSpec((tm, 2 * KV_W), row),
            pl.BlockSpec((tm, C_CONV), row),
            pl.BlockSpec((tm, 2 * D_MODEL), row),
        ],
        out_shape=[
            jax.ShapeDtypeStruct((n, Q_W), BF16),
            jax.ShapeDtypeStruct((n, KV_W), F32),
            jax.ShapeDtypeStruct((n, KV_W), F32),
            jax.ShapeDtypeStruct((n, 2 * KV_W), BF16),
            jax.ShapeDtypeStruct((n, 2 * KV_W), BF16),
            jax.ShapeDtypeStruct((n, C_CONV), F32),
            jax.ShapeDtypeStruct((n, 2 * D_MODEL), BF16),
        ],
        scratch_shapes=[pltpu.VMEM((tm, D_MODEL), BF16)],
        compiler_params=pltpu.CompilerParams(dimension_semantics=("arbitrary",),
                                             vmem_limit_bytes=VMEM_LIMIT),
        name="in_proj",
    )(x2d, ln1, w_in, b_glu, qg, kg, cos, sin, seg)


def _attn_kernel(sinks_ref, q_ref, kp_ref, kc_ref, vp_ref, vc_ref, o_ref):
    step = pl.program_id(1)
    bq = WINDOW
    nk = 2 * bq
    qrow = lax.broadcasted_iota(jnp.int32, (bq, nk), 0)
    col = lax.broadcasted_iota(jnp.int32, (bq, nk), 1)
    band = (col > qrow) & (col <= qrow + WINDOW)
    base_first = jnp.where(band & ((col >= bq) | (step > 0)), 0.0, NEG)
    base_rest = jnp.where(band, 0.0, NEG)
    lane = lax.broadcasted_iota(jnp.int32, (bq, LANES), 1)
    sink_slot = lane == 0
    low = lane < HEAD_DIM
    first_row = lax.broadcasted_iota(jnp.int32, (BF16_ROWS, LANES), 0) == 0
    ones = jnp.ones((nk, LANES), BF16)

    def window(prev_ref, p0, cur_ref, c0, lanes):
        head = prev_ref[p0:p0 + BF16_ROWS, lanes]
        head = jnp.where(first_row, jnp.zeros_like(head), head)
        return jnp.concatenate([head, prev_ref[p0 + BF16_ROWS:p0 + bq, lanes], cur_ref[c0:c0 + bq, lanes]],
                               axis=0)

    for sub in range(q_ref.shape[0] // bq):
        rows = slice(sub * bq, (sub + 1) * bq)
        base = base_first if sub == 0 else base_rest
        k_prev, v_prev, p0 = (kp_ref, vp_ref, 0) if sub == 0 else (kc_ref, vc_ref, (sub - 1) * bq)
        for h in range(N_KV_HEADS):
            lanes = slice(h * LANES, (h + 1) * LANES)
            kmat = window(k_prev, p0, kc_ref, sub * bq, lanes)
            vmat = window(v_prev, p0, vc_ref, sub * bq, lanes)
            vrhs = jnp.concatenate([vmat, ones], axis=1)
            q_parts = []
            bias_parts = []
            for g in range(GROUP):
                qc = q_ref[rows, (2 * h + g // 2) * LANES:(2 * h + g // 2 + 1) * LANES]
                zero = jnp.zeros_like(qc)
                q_parts.append(jnp.where(low, qc, zero) if g % 2 == 0 else jnp.where(low, zero, qc))
                sink = sinks_ref[GROUP * h + g] * LOG2E
                bias_parts.append(jnp.concatenate(
                    [jnp.where(sink_slot, sink, base[:, 0:LANES]), base[:, LANES:]], axis=1))
            s = _dot_nt(jnp.concatenate(q_parts, axis=0), kmat) + jnp.concatenate(bias_parts, axis=0)
            m = jnp.max(s, axis=-1, keepdims=True)
            e = jnp.exp2(s - m).astype(BF16)
            pvd = _dot(e, vrhs)
            o = pvd[:, 0:LANES] / pvd[:, LANES:2 * LANES]
            for half in range(2):
                r0 = 2 * half * bq
                o_ref[rows, (2 * h + half) * LANES:(2 * h + half + 1) * LANES] = jnp.where(
                    low, o[r0:r0 + bq], o[r0 + bq:r0 + 2 * bq]).astype(BF16)


ATTN_BLOCKS = 2


def _attention(q, kd, vd, sinks, batch, seq):
    bq = WINDOW
    kvw = N_KV_HEADS * LANES
    nb = seq // bq
    ns = nb // ATTN_BLOCKS
    cur = lambda b, i: (b * ns + i, 0)
    prev = lambda b, i: (b * nb + jnp.maximum(i * ATTN_BLOCKS - 1, 0), 0)
    return pl.pallas_call(
        _attn_kernel,
        grid=(batch, ns),
        in_specs=[
            pl.BlockSpec(memory_space=pltpu.SMEM),
            pl.BlockSpec((ATTN_BLOCKS * bq, Q_W), cur),
            pl.BlockSpec((bq, kvw), prev),
            pl.BlockSpec((ATTN_BLOCKS * bq, kvw), cur),
            pl.BlockSpec((bq, kvw), prev),
            pl.BlockSpec((ATTN_BLOCKS * bq, kvw), cur),
        ],
        out_specs=pl.BlockSpec((ATTN_BLOCKS * bq, Q_W), cur),
        out_shape=jax.ShapeDtypeStruct((batch * seq, Q_W), BF16),
        compiler_params=pltpu.CompilerParams(dimension_semantics=("arbitrary", "arbitrary")),
        name="swa_attention",
    )(sinks, q, kd, kd, vd, vd)


def _sample_attn_kernel(sinks_ref, qb_ref, knew_ref, vnew_ref, ck_ref, cv_ref,
                        nk_ref, nv_ref, o_ref):
    nb, wb, _ = ck_ref.shape
    sink = sinks_ref[...] * LOG2E
    for b in range(nb):
        nk_ref[b, 0:wb - 1, :] = ck_ref[b, 1:wb, :]
        nk_ref[b, wb - 1:wb, :] = knew_ref[b]
        nv_ref[b, 0:wb - 1, :] = cv_ref[b, 1:wb, :]
        nv_ref[b, wb - 1:wb, :] = vnew_ref[b]
        keys = nk_ref[b].astype(BF16)
        vals = nv_ref[b].astype(BF16)
        s = _dot_nt(qb_ref[b], keys)
        m = jnp.maximum(jnp.max(s, axis=-1, keepdims=True), sink)
        e = jnp.exp2(s - m)
        den = jnp.sum(e, axis=-1, keepdims=True) + jnp.exp2(sink - m)
        o_ref[b] = _dot(e.astype(BF16), vals) / den


def _sample_attention(qb, k_new, v_new, cache_k, cache_v, sinks_col, nb):
    bd, wb, _ = cache_k.shape
    blk3 = lambda i: (i, 0, 0)
    return pl.pallas_call(
        _sample_attn_kernel,
        grid=(bd // nb,),
        in_specs=[
            _const_spec((N_HEADS, 1)),
            pl.BlockSpec((nb, N_HEADS, KV_W), blk3),
            pl.BlockSpec((nb, 1, KV_W), blk3),
            pl.BlockSpec((nb, 1, KV_W), blk3),
            pl.BlockSpec((nb, wb, KV_W), blk3),
            pl.BlockSpec((nb, wb, KV_W), blk3),
        ],
        out_specs=[
            pl.BlockSpec((nb, wb, KV_W), blk3),
            pl.BlockSpec((nb, wb, KV_W), blk3),
            pl.BlockSpec((nb, N_HEADS, KV_W), blk3),
        ],
        out_shape=[
            jax.ShapeDtypeStruct((bd, wb, KV_W), F32),
            jax.ShapeDtypeStruct((bd, wb, KV_W), F32),
            jax.ShapeDtypeStruct((bd, N_HEADS, KV_W), F32),
        ],
        compiler_params=pltpu.CompilerParams(dimension_semantics=("arbitrary",)),
        name="sample_attention",
    )(sinks_col, qb, k_new, v_new, cache_k, cache_v)


def _sample_conv_kernel(a_ref, st_ref, dw_ref, dwb_ref, y_ref, ns_ref):
    nb, hist, _ = st_ref.shape
    w_hist = dw_ref[0:hist, :]
    w_last = dw_ref[hist:hist + 1, :]
    for b in range(nb):
        a_new = a_ref[b]
        y_ref[b] = (jnp.sum(st_ref[b] * w_hist, axis=0, keepdims=True)
                    + a_new * w_last + dwb_ref[...])
        ns_ref[b, 0:hist - 1, :] = st_ref[b, 1:hist, :]
        ns_ref[b, hist - 1:hist, :] = a_new


def _sample_conv(a_new, state, conv_dw, conv_dw_b, nb):
    bd, hist, c = state.shape
    blk3 = lambda i: (i, 0, 0)
    return pl.pallas_call(
        _sample_conv_kernel,
        grid=(bd // nb,),
        in_specs=[
            pl.BlockSpec((nb, 1, c), blk3),
            pl.BlockSpec((nb, hist, c), blk3),
            _const_spec((CONV_W, c)),
            _const_spec((1, c)),
        ],
        out_specs=[
            pl.BlockSpec((nb, 1, c), blk3),
            pl.BlockSpec((nb, hist, c), blk3),
        ],
        out_shape=[
            jax.ShapeDtypeStruct((bd, 1, c), F32),
            jax.ShapeDtypeStruct((bd, hist, c), F32),
        ],
        compiler_params=pltpu.CompilerParams(dimension_semantics=("arbitrary",)),
        name="sample_conv",
    )(a_new, state, conv_dw, conv_dw_b)


CONV_ROWS = 64
N_CHUNKS = C_CONV // LANES
PITCH = N_CHUNKS + 1


def _hist_store(hist_ref, t0, rows, value):
    for c in range(N_CHUNKS):
        hist_ref[pl.ds(t0 * PITCH + c, rows, stride=PITCH), :] = value[:, c * LANES:(c + 1) * LANES]


def _conv_tile(hist_ref, dw_ref, dwb_ref, y_ref, tm):
    lead = HALO - (CONV_W - 1)
    groups = CONV_ROWS // SUBLANES

    def body(rb, carry):
        r0 = pl.multiple_of(rb * CONV_ROWS, CONV_ROWS)
        for c in range(N_CHUNKS):
            lanes = slice(c * LANES, (c + 1) * LANES)
            taps = [jnp.broadcast_to(dw_ref[j:j + 1, lanes], (SUBLANES, LANES)) for j in range(CONV_W)]
            accs = [jnp.broadcast_to(dwb_ref[:, lanes], (SUBLANES, LANES))] * groups
            for o in range(CONV_ROWS + CONV_W - SUBLANES):
                win = hist_ref[pl.ds((r0 + lead + o) * PITCH + c, SUBLANES, stride=PITCH), :]
                for k in range(groups):
                    j = o - SUBLANES * k
                    if 0 <= j < CONV_W:
                        accs[k] = accs[k] + taps[j] * win
            for k in range(groups):
                y_ref[pl.ds(pl.multiple_of(r0 + SUBLANES * k, SUBLANES), SUBLANES), lanes] = accs[k]
        return carry

    lax.fori_loop(0, tm // CONV_ROWS, body, 0)


LN_ROWS = 64


def _ln_swish(y_ref, act_ref, lng, lnb):
    for r0 in range(0, y_ref.shape[0], LN_ROWS):
        y = y_ref[r0:r0 + LN_ROWS, :]
        d = y - jnp.mean(y, axis=-1, keepdims=True)
        var = jnp.mean(d * d, axis=-1, keepdims=True)
        yn = d * lax.rsqrt(var + EPS) * lng + lnb
        act_ref[r0:r0 + LN_ROWS, :] = (yn * jax.nn.sigmoid(yn)).astype(BF16)


def _mix_project(act, x, attn, gates, wco, bco, wo, wout):
    conv_o = _dot(act, wco) + bco
    attn_p = _dot(attn, wo)
    mixed = (gates[:, 0:D_MODEL].astype(F32) * attn_p
             + gates[:, D_MODEL:2 * D_MODEL].astype(F32) * conv_o)
    return x + _dot(mixed.astype(BF16), wout)


def _mix_prompt_kernel(x_ref, a_ref, halo_ref, attn_ref, g_ref, dw_ref, dwb_ref, lng_ref, lnb_ref,
                       wco_ref, bco_ref, wo_ref, wout_ref, h_ref, hist_ref, y_ref, act_ref):
    tm = x_ref.shape[0]
    seq_start = pl.program_id(1) == 0
    _hist_store(hist_ref, 0, HALO, jnp.where(seq_start, 0.0, halo_ref[...]))
    _hist_store(hist_ref, HALO, tm, a_ref[...])
    _conv_tile(hist_ref, dw_ref, dwb_ref, y_ref, tm)
    _ln_swish(y_ref, act_ref, lng_ref[...], lnb_ref[...])
    h_ref[...] = _mix_project(act_ref[...], x_ref[...], attn_ref[...], g_ref[...],
                              wco_ref[...], bco_ref[...], wo_ref[...], wout_ref[...])


def _mix_sample_kernel(x_ref, y_ref, attn_ref, g_ref, lng_ref, lnb_ref,
                       wco_ref, bco_ref, wo_ref, wout_ref, h_ref, act_ref):
    _ln_swish(y_ref, act_ref, lng_ref[...], lnb_ref[...])
    h_ref[...] = _mix_project(act_ref[...], x_ref[...], attn_ref[...], g_ref[...],
                              wco_ref[...], bco_ref[...], wo_ref[...], wout_ref[...])


def _mix_prompt(x2d, a, attn, gates, batch, seq, tm, dw, dwb, lng, lnb, wco, bco, wo, wout):
    nt = seq // tm
    row = lambda b, i: (b * nt + i, 0)
    halo = lambda b, i: (b * (seq // HALO) + jnp.maximum(i * (tm // HALO) - 1, 0), 0)
    sq = (D_MODEL, D_MODEL)
    return pl.pallas_call(
        _mix_prompt_kernel,
        grid=(batch, nt),
        in_specs=[
            pl.BlockSpec((tm, D_MODEL), row),
            pl.BlockSpec((tm, C_CONV), row),
            pl.BlockSpec((HALO, C_CONV), halo),
            pl.BlockSpec((tm, Q_W), row),
            pl.BlockSpec((tm, 2 * D_MODEL), row),
            _const_spec((CONV_W, C_CONV)),
            _const_spec((1, C_CONV)),
            _const_spec((1, C_CONV)),
            _const_spec((1, C_CONV)),
            _const_spec(sq),
            _const_spec((1, D_MODEL)),
            _const_spec(sq),
            _const_spec(sq),
        ],
        out_specs=pl.BlockSpec((tm, D_MODEL), row),
        out_shape=jax.ShapeDtypeStruct((batch * seq, D_MODEL), F32),
        scratch_shapes=[pltpu.VMEM(((HALO + tm) * PITCH, LANES), F32), pltpu.VMEM((tm, C_CONV), F32),
                        pltpu.VMEM((tm, C_CONV), BF16)],
        compiler_params=pltpu.CompilerParams(dimension_semantics=("arbitrary", "arbitrary"),
                                             vmem_limit_bytes=VMEM_LIMIT),
        name="mix_prompt",
    )(x2d, a, a, attn, gates, dw, dwb, lng, lnb, wco, bco, wo, wout)


def _mix_sample(x2d, y, attn, gates, lng, lnb, wco, bco, wo, wout):
    n = x2d.shape[0]
    sq = (D_MODEL, D_MODEL)
    full = lambda w: pl.BlockSpec((n, w), lambda i: (0, 0))
    return pl.pallas_call(
        _mix_sample_kernel,
        grid=(1,),
        in_specs=[full(D_MODEL), full(C_CONV), full(Q_W), full(2 * D_MODEL),
                  _const_spec((1, C_CONV)), _const_spec((1, C_CONV)),
                  _const_spec(sq), _const_spec((1, D_MODEL)), _const_spec(sq), _const_spec(sq)],
        out_specs=full(D_MODEL),
        out_shape=jax.ShapeDtypeStruct((n, D_MODEL), F32),
        scratch_shapes=[pltpu.VMEM((n, C_CONV), BF16)],
        compiler_params=pltpu.CompilerParams(dimension_semantics=("arbitrary",),
                                             vmem_limit_bytes=VMEM_LIMIT),
        name="mix_sample",
    )(x2d, y, attn, gates, lng, lnb, wco, bco, wo, wout)


FF_CHUNK = 1024


def _ffn_kernel(h_ref, p_ref, ln2_ref, w1_ref, w2_ref, lnp_ref, wg_ref, wp_ref, o_ref, u_ref, acc_ref):
    h = h_ref[...]
    u_ref[...] = (h * _rms_scale(h) * ln2_ref[...]).astype(BF16)
    acc_ref[...] = h
    for c0 in range(0, D_FF, FF_CHUNK):
        hid = jnp.maximum(_dot(u_ref[...], w1_ref[:, c0:c0 + FF_CHUNK]), 0.0)
        acc_ref[...] += _dot((hid * hid).astype(BF16), w2_ref[c0:c0 + FF_CHUNK, :])
    h2 = acc_ref[...]
    un = (h2 * _rms_scale(h2) * lnp_ref[...]).astype(BF16)
    gate = jax.nn.sigmoid(_dot(un, wg_ref[...]))
    o_ref[...] = h2 + gate * _dot(p_ref[...].astype(BF16), wp_ref[...])


def _ffn(h, p2d, tm, ln2, w1, w2, lnp, wg, wp):
    n = h.shape[0]
    d_ple = p2d.shape[1]
    row = lambda i: (i, 0)
    return pl.pallas_call(
        _ffn_kernel,
        grid=(n // tm,),
        in_specs=[
            pl.BlockSpec((tm, D_MODEL), row),
            pl.BlockSpec((tm, d_ple), row),
            _const_spec((1, D_MODEL)),
            _const_spec((D_MODEL, D_FF)),
            _const_spec((D_FF, D_MODEL)),
            _const_spec((1, D_MODEL)),
            _const_spec((D_MODEL, D_MODEL)),
            _const_spec((d_ple, D_MODEL)),
        ],
        out_specs=pl.BlockSpec((tm, D_MODEL), row),
        out_shape=jax.ShapeDtypeStruct((n, D_MODEL), F32),
        scratch_shapes=[pltpu.VMEM((tm, D_MODEL), BF16), pltpu.VMEM((tm, D_MODEL), F32)],
        compiler_params=pltpu.CompilerParams(dimension_semantics=("arbitrary",),
                                             vmem_limit_bytes=VMEM_LIMIT),
        name="ffn",
    )(h, p2d, ln2, w1, w2, lnp, wg, wp)


def _rope_tables(pos):
    half = HEAD_DIM // 2
    inv = jnp.power(jnp.float32(ROPE_THETA), -jnp.arange(half, dtype=F32) / half)
    ang = pos.astype(F32)[:, None] * inv[None, :]
    cos = jnp.tile(jnp.cos(ang), (1, LANES // half))
    sign = jnp.tile(jnp.concatenate([-jnp.ones((half,), F32), jnp.ones((half,), F32)]), LANES // HEAD_DIM)
    sin = jnp.tile(jnp.sin(ang), (1, LANES // half)) * sign[None, :]
    return cos, sin


def _row(v):
    return v.reshape(1, -1)


def kernel(x_prompt, x_sample, cache_k, cache_v, state_conv, p_prompt, p_sample, ln1, w_in, b_glu, q_norm, k_norm, sinks, w_o_attn, conv_dw, conv_dw_b, conv_ln_g, conv_ln_b, w_conv_out, b_conv_out, w_out, ln2, w_ff1, w_ff2, ln_ple, w_ple_gate, w_ple):
    depth = w_in.shape[0]
    batch, seq, _ = x_prompt.shape
    bd, dseq, _ = x_sample.shape
    wb = cache_k.shape[2]
    hist = state_conv.shape[2]
    assert dseq == 1 and wb == WINDOW and hist == CONV_W - 1 and seq % WINDOW == 0

    tm = 512
    seg = (jnp.arange(MXU_N)[:, None] // HEAD_DIM == jnp.arange(MXU_N)[None, :] // HEAD_DIM).astype(BF16)
    cos_p, sin_p = _rope_tables(jnp.arange(seq, dtype=jnp.int32))
    cos_s, sin_s = _rope_tables(jnp.full((bd,), PAST_LEN, dtype=jnp.int32))
    head_ids = jnp.arange(N_HEADS)

    xp = x_prompt.reshape(batch * seq, D_MODEL)
    xs = x_sample.reshape(bd, D_MODEL)
    outs = [[] for _ in range(6)]
    for i in range(depth):
        win = w_in[i].astype(BF16)
        qg = _row(jnp.tile(q_norm[i], N_HEADS))
        kg = _row(jnp.tile(k_norm[i], N_KV_HEADS))
        wco = w_conv_out[i].astype(BF16)
        wo = w_o_attn[i].astype(BF16)
        wout = w_out[i].astype(BF16)
        w1 = w_ff1[i].astype(BF16)
        w2 = w_ff2[i].astype(BF16)
        wg = w_ple_gate[i].astype(BF16)
        wp = w_ple[i].astype(BF16)
        inproj_w = (_row(ln1[i]), win, _row(b_glu[i]), qg, kg, seg)
        post_w = (_row(conv_ln_g[i]), _row(conv_ln_b[i]), wco, _row(b_conv_out[i]), wo, wout)
        ffn_w = (_row(ln2[i]), w1, w2, _row(ln_ple[i]), wg, wp)

        q, k, v, kd, vd, a, g = _inproj(xp, cos_p, sin_p, seq // tm, tm, *inproj_w)
        attn = _attention(q, kd, vd, sinks[i], batch, seq)
        h = _mix_prompt(xp, a, attn, g, batch, seq, tm, conv_dw[i], _row(conv_dw_b[i]), *post_w)
        xp = _ffn(h, p_prompt[i].reshape(batch * seq, -1), tm, *ffn_w)
        outs[0].append(k.reshape(batch, seq, N_KV_HEADS, HEAD_DIM)[:, seq - WINDOW:])
        outs[1].append(v.reshape(batch, seq, N_KV_HEADS, HEAD_DIM)[:, seq - WINDOW:])
        outs[2].append(a.reshape(batch, seq, C_CONV)[:, seq - hist:])

        qs, ks, vs, _, _, a_s, gs = _inproj(xs, cos_s, sin_s, 1, bd, *inproj_w)
        qb = (jnp.zeros((bd, N_HEADS, N_KV_HEADS, HEAD_DIM), BF16)
              .at[:, head_ids, head_ids // GROUP].set(qs.reshape(bd, N_HEADS, HEAD_DIM))
              .reshape(bd, N_HEADS, KV_W))
        nk, nv, ao = _sample_attention(qb, ks.reshape(bd, 1, KV_W), vs.reshape(bd, 1, KV_W),
                                       cache_k[i].reshape(bd, wb, KV_W), cache_v[i].reshape(bd, wb, KV_W),
                                       sinks[i].reshape(N_HEADS, 1), 8)
        attn_s = (ao.reshape(bd, N_HEADS, N_KV_HEADS, HEAD_DIM)[:, head_ids, head_ids // GROUP]
                  .reshape(bd, Q_W).astype(BF16))
        y_s, ns = _sample_conv(a_s.reshape(bd, 1, C_CONV), state_conv[i], conv_dw[i], _row(conv_dw_b[i]), 8)
        hs = _mix_sample(xs, y_s.reshape(bd, C_CONV), attn_s, gs, *post_w)
        xs = _ffn(hs, p_sample[i].reshape(bd, -1), bd, *ffn_w)
        outs[3].append(nk.reshape(bd, wb, N_KV_HEADS, HEAD_DIM))
        outs[4].append(nv.reshape(bd, wb, N_KV_HEADS, HEAD_DIM))
        outs[5].append(ns)

    return (xp.reshape(batch, seq, D_MODEL), xs.reshape(bd, dseq, D_MODEL),
            jnp.stack(outs[0]), jnp.stack(outs[1]), jnp.stack(outs[2]),
            jnp.stack(outs[3]), jnp.stack(outs[4]), jnp.stack(outs[5]))
```

```python
import functools

import jax
import jax.numpy as jnp
from jax import lax
from jax.experimental import pallas as pl
from jax.experimental.pallas import tpu as pltpu

D_MODEL = 1024
N_HEADS = 16
N_KV_HEADS = 4
HEAD_DIM = 64
GROUP = N_HEADS // N_KV_HEADS
WINDOW = 128
ROPE_THETA = 10000.0
PAST_LEN = 16384
C_CONV = D_MODEL
CONV_W = 31
D_FF = 4 * D_MODEL
EPS = 1e-6
NEG = -1e30
Q_W = N_HEADS * HEAD_DIM
KV_W = N_KV_HEADS * HEAD_DIM
D_IN = Q_W + 2 * KV_W + 2 * C_CONV + 2 * D_MODEL

LOG2E = 1.4426950408889634
Q_SCALE = HEAD_DIM ** -0.5 * LOG2E

LANES = 128
SUBLANES = 8
BF16_ROWS = 16
MXU_N = 256
HALO = 32
VMEM_LIMIT = 56 * 1024 * 1024

F32 = jnp.float32
BF16 = jnp.bfloat16


def _dot(a, b):
    return jnp.dot(a, b, preferred_element_type=F32)


def _dot_nt(a, b):
    return lax.dot_general(a, b, (((1,), (1,)), ((), ())), preferred_element_type=F32)


def _const_spec(shape):
    nd = len(shape)
    return pl.BlockSpec(shape, lambda *_: (0,) * nd, pipeline_mode=pl.Buffered(1))


def _rms_scale(x):
    return lax.rsqrt(jnp.mean(x * x, axis=-1, keepdims=True) + EPS)


def _head_norm_rope(z, gain, seg, cos, sin_signed, scale):
    tm, width = z.shape
    outs = []
    lane = lax.broadcasted_iota(jnp.int32, (tm, LANES), 1)
    first_half = (lane & (HEAD_DIM - 1)) < (HEAD_DIM // 2)
    for c0 in range(0, width, MXU_N):
        zc = z[:, c0:c0 + MXU_N]
        ss = _dot((zc * zc).astype(BF16), seg)
        zn = zc * lax.rsqrt(ss * (1.0 / HEAD_DIM) + EPS) * gain[:, c0:c0 + MXU_N]
        for l0 in range(0, MXU_N, LANES):
            xc = zn[:, l0:l0 + LANES]
            partner = jnp.where(first_half,
                                pltpu.roll(xc, LANES - HEAD_DIM // 2, axis=1),
                                pltpu.roll(xc, HEAD_DIM // 2, axis=1))
            outs.append((xc * cos + partner * sin_signed) * scale)
    return outs


def _store_head_pairs(dup_ref, i, xc):
    low = lax.broadcasted_iota(jnp.int32, xc.shape, 1) < HEAD_DIM
    rolled = pltpu.roll(xc, HEAD_DIM, axis=1)
    dup_ref[:, (2 * i) * LANES:(2 * i + 1) * LANES] = jnp.where(low, xc, rolled).astype(BF16)
    dup_ref[:, (2 * i + 1) * LANES:(2 * i + 2) * LANES] = jnp.where(low, rolled, xc).astype(BF16)


def _inproj_kernel(x_ref, ln1_ref, w_ref, bglu_ref, qg_ref, kg_ref, cos_ref, sin_ref, seg_ref,
                   q_ref, k_ref, v_ref, kd_ref, vd_ref, a_ref, g_ref, u_ref):
    x = x_ref[...]
    u_ref[...] = (x * _rms_scale(x) * ln1_ref[...]).astype(BF16)
    cos = cos_ref[...]
    sin = sin_ref[...]
    seg = seg_ref[...]

    zq = _dot(u_ref[...], w_ref[:, 0:Q_W])
    for i, qc in enumerate(_head_norm_rope(zq, qg_ref[...], seg, cos, sin, Q_SCALE)):
        q_ref[:, i * LANES:(i + 1) * LANES] = qc.astype(BF16)

    zkv = _dot(u_ref[...], w_ref[:, Q_W:Q_W + 2 * KV_W])
    zv = zkv[:, KV_W:2 * KV_W]
    for i, kc in enumerate(_head_norm_rope(zkv[:, 0:KV_W], kg_ref[...], seg, cos, sin, 1.0)):
        k_ref[:, i * LANES:(i + 1) * LANES] = kc
        _store_head_pairs(kd_ref, i, kc)
    v_ref[...] = zv
    for i in range(KV_W // LANES):
        _store_head_pairs(vd_ref, i, zv[:, i * LANES:(i + 1) * LANES])

    glu0 = Q_W + 2 * KV_W
    lin = _dot(u_ref[...], w_ref[:, glu0:glu0 + C_CONV]) + bglu_ref[:, 0:C_CONV]
    gate = _dot(u_ref[...], w_ref[:, glu0 + C_CONV:glu0 + 2 * C_CONV]) + bglu_ref[:, C_CONV:2 * C_CONV]
    a_ref[...] = lin * jax.nn.sigmoid(gate)

    g0 = glu0 + 2 * C_CONV
    for c0 in range(0, 2 * D_MODEL, D_MODEL):
        g_ref[:, c0:c0 + D_MODEL] = jax.nn.sigmoid(_dot(u_ref[...], w_ref[:, g0 + c0:g0 + c0 + D_MODEL])).astype(BF16)


def _inproj(x2d, cos, sin, pos_tiles, tm, ln1, w_in, b_glu, qg, kg, seg):
    n = x2d.shape[0]
    grid = (n // tm,)
    row = lambda i: (i, 0)
    pos = lambda i: (i % pos_tiles, 0)
    return pl.pallas_call(
        _inproj_kernel,
        grid=grid,
        in_specs=[
            pl.BlockSpec((tm, D_MODEL), row),
            _const_spec((1, D_MODEL)),
            _const_spec((D_MODEL, D_IN)),
            _const_spec((1, 2 * C_CONV)),
            _const_spec((1, Q_W)),
            _const_spec((1, KV_W)),
            pl.BlockSpec((tm, LANES), pos),
            pl.BlockSpec((tm, LANES), pos),
            _const_spec((MXU_N, MXU_N)),
        ],
        out_specs=[
            pl.BlockSpec((tm, Q_W), row),
            pl.BlockSpec((tm, KV_W), row),
            pl.BlockSpec((tm, KV_W), row),
            pl.BlockSpec((tm, 2 * KV_W), row),
            pl.BlockSpec((tm, 2 * KV_W), row),
            pl.BlockSpec((tm, C_CONV), row),
            pl.BlockSpec((tm, 2 * D_MODEL), row),
        ],
        out_shape=[
            jax.ShapeDtypeStruct((n, Q_W), BF16),
            jax.ShapeDtypeStruct((n, KV_W), F32),
            jax.ShapeDtypeStruct((n, KV_W), F32),
            jax.ShapeDtypeStruct((n, 2 * KV_W), BF16),
            jax.ShapeDtypeStruct((n, 2 * KV_W), BF16),
            jax.ShapeDtypeStruct((n, C_CONV), F32),
            jax.ShapeDtypeStruct((n, 2 * D_MODEL), BF16),
        ],
        scratch_shapes=[pltpu.VMEM((tm, D_MODEL), BF16)],
        compiler_params=pltpu.CompilerParams(dimension_semantics=("arbitrary",),
                                             vmem_limit_bytes=VMEM_LIMIT),
        name="in_proj",
    )(x2d, ln1, w_in, b_glu, qg, kg, cos, sin, seg)


def _attn_kernel(sinks_ref, q_ref, kp_ref, kc_ref, vp_ref, vc_ref, o_ref):
    step = pl.program_id(1)
    bq = WINDOW
    nk = 2 * bq
    qrow = lax.broadcasted_iota(jnp.int32, (bq, nk), 0)
    col = lax.broadcasted_iota(jnp.int32, (bq, nk), 1)
    band = (col > qrow) & (col <= qrow + WINDOW)
    base_first = jnp.where(band & ((col >= bq) | (step > 0)), 0.0, NEG)
    base_rest = jnp.where(band, 0.0, NEG)
    lane = lax.broadcasted_iota(jnp.int32, (bq, LANES), 1)
    sink_slot = lane == 0
    low = lane < HEAD_DIM
    first_row = lax.broadcasted_iota(jnp.int32, (BF16_ROWS, LANES), 0) == 0
    ones = jnp.ones((nk, LANES), BF16)

    def window(prev_ref, p0, cur_ref, c0, lanes):
        head = prev_ref[p0:p0 + BF16_ROWS, lanes]
        head = jnp.where(first_row, jnp.zeros_like(head), head)
        return jnp.concatenate([head, prev_ref[p0 + BF16_ROWS:p0 + bq, lanes], cur_ref[c0:c0 + bq, lanes]],
                               axis=0)

    for sub in range(q_ref.shape[0] // bq):
        rows = slice(sub * bq, (sub + 1) * bq)
        base = base_first if sub == 0 else base_rest
        k_prev, v_prev, p0 = (kp_ref, vp_ref, 0) if sub == 0 else (kc_ref, vc_ref, (sub - 1) * bq)
        for h in range(N_KV_HEADS):
            lanes = slice(h * LANES, (h + 1) * LANES)
            kmat = window(k_prev, p0, kc_ref, sub * bq, lanes)
            vmat = window(v_prev, p0, vc_ref, sub * bq, lanes)
            vrhs = jnp.concatenate([vmat, ones], axis=1)
            q_parts = []
            bias_parts = []
            for g in range(GROUP):
                qc = q_ref[rows, (2 * h + g // 2) * LANES:(2 * h + g // 2 + 1) * LANES]
                zero = jnp.zeros_like(qc)
                q_parts.append(jnp.where(low, qc, zero) if g % 2 == 0 else jnp.where(low, zero, qc))
                sink = sinks_ref[GROUP * h + g] * LOG2E
                bias_parts.append(jnp.concatenate(
                    [jnp.where(sink_slot, sink, base[:, 0:LANES]), base[:, LANES:]], axis=1))
            s = _dot_nt(jnp.concatenate(q_parts, axis=0), kmat) + jnp.concatenate(bias_parts, axis=0)
            m = jnp.max(s, axis=-1, keepdims=True)
            e = jnp.exp2(s - m).astype(BF16)
            pvd = _dot(e, vrhs)
            o = pvd[:, 0:LANES] / pvd[:, LANES:2 * LANES]
            for half in range(2):
                r0 = 2 * half * bq
                o_ref[rows, (2 * h + half) * LANES:(2 * h + half + 1) * LANES] = jnp.where(
                    low, o[r0:r0 + bq], o[r0 + bq:r0 + 2 * bq]).astype(BF16)


ATTN_BLOCKS = 4


def _attention(q, kd, vd, sinks, batch, seq):
    bq = WINDOW
    kvw = N_KV_HEADS * LANES
    nb = seq // bq
    ns = nb // ATTN_BLOCKS
    cur = lambda b, i: (b * ns + i, 0)
    prev = lambda b, i: (b * nb + jnp.maximum(i * ATTN_BLOCKS - 1, 0), 0)
    return pl.pallas_call(
        _attn_kernel,
        grid=(batch, ns),
        in_specs=[
            pl.BlockSpec(memory_space=pltpu.SMEM),
            pl.BlockSpec((ATTN_BLOCKS * bq, Q_W), cur),
            pl.BlockSpec((bq, kvw), prev),
            pl.BlockSpec((ATTN_BLOCKS * bq, kvw), cur),
            pl.BlockSpec((bq, kvw), prev),
            pl.BlockSpec((ATTN_BLOCKS * bq, kvw), cur),
        ],
        out_specs=pl.BlockSpec((ATTN_BLOCKS * bq, Q_W), cur),
        out_shape=jax.ShapeDtypeStruct((batch * seq, Q_W), BF16),
        compiler_params=pltpu.CompilerParams(dimension_semantics=("arbitrary", "arbitrary")),
        name="swa_attention",
    )(sinks, q, kd, kd, vd, vd)


def _sample_attn_kernel(sinks_ref, qb_ref, knew_ref, vnew_ref, ck_ref, cv_ref,
                        nk_ref, nv_ref, o_ref):
    nb, wb, _ = ck_ref.shape
    sink = sinks_ref[...] * LOG2E
    for b in range(nb):
        nk_ref[b, 0:wb - 1, :] = ck_ref[b, 1:wb, :]
        nk_ref[b, wb - 1:wb, :] = knew_ref[b]
        nv_ref[b, 0:wb - 1, :] = cv_ref[b, 1:wb, :]
        nv_ref[b, wb - 1:wb, :] = vnew_ref[b]
        keys = nk_ref[b].astype(BF16)
        vals = nv_ref[b].astype(BF16)
        s = _dot_nt(qb_ref[b], keys)
        m = jnp.maximum(jnp.max(s, axis=-1, keepdims=True), sink)
        e = jnp.exp2(s - m)
        den = jnp.sum(e, axis=-1, keepdims=True) + jnp.exp2(sink - m)
        o_ref[b] = _dot(e.astype(BF16), vals) / den


def _sample_attention(qb, k_new, v_new, cache_k, cache_v, sinks_col, nb):
    bd, wb, _ = cache_k.shape
    blk3 = lambda i: (i, 0, 0)
    return pl.pallas_call(
        _sample_attn_kernel,
        grid=(bd // nb,),
        in_specs=[
            _const_spec((N_HEADS, 1)),
            pl.BlockSpec((nb, N_HEADS, KV_W), blk3),
            pl.BlockSpec((nb, 1, KV_W), blk3),
            pl.BlockSpec((nb, 1, KV_W), blk3),
            pl.BlockSpec((nb, wb, KV_W), blk3),
            pl.BlockSpec((nb, wb, KV_W), blk3),
        ],
        out_specs=[
            pl.BlockSpec((nb, wb, KV_W), blk3),
            pl.BlockSpec((nb, wb, KV_W), blk3),
            pl.BlockSpec((nb, N_HEADS, KV_W), blk3),
        ],
        out_shape=[
            jax.ShapeDtypeStruct((bd, wb, KV_W), F32),
            jax.ShapeDtypeStruct((bd, wb, KV_W), F32),
            jax.ShapeDtypeStruct((bd, N_HEADS, KV_W), F32),
        ],
        compiler_params=pltpu.CompilerParams(dimension_semantics=("arbitrary",)),
        name="sample_attention",
    )(sinks_col, qb, k_new, v_new, cache_k, cache_v)


def _sample_conv_kernel(a_ref, st_ref, dw_ref, dwb_ref, y_ref, ns_ref):
    nb, hist, _ = st_ref.shape
    w_hist = dw_ref[0:hist, :]
    w_last = dw_ref[hist:hist + 1, :]
    for b in range(nb):
        a_new = a_ref[b]
        y_ref[b] = (jnp.sum(st_ref[b] * w_hist, axis=0, keepdims=True)
                    + a_new * w_last + dwb_ref[...])
        ns_ref[b, 0:hist - 1, :] = st_ref[b, 1:hist, :]
        ns_ref[b, hist - 1:hist, :] = a_new


def _sample_conv(a_new, state, conv_dw, conv_dw_b, nb):
    bd, hist, c = state.shape
    blk3 = lambda i: (i, 0, 0)
    return pl.pallas_call(
        _sample_conv_kernel,
        grid=(bd // nb,),
        in_specs=[
            pl.BlockSpec((nb, 1, c), blk3),
            pl.BlockSpec((nb, hist, c), blk3),
            _const_spec((CONV_W, c)),
            _const_spec((1, c)),
        ],
        out_specs=[
            pl.BlockSpec((nb, 1, c), blk3),
            pl.BlockSpec((nb, hist, c), blk3),
        ],
        out_shape=[
            jax.ShapeDtypeStruct((bd, 1, c), F32),
            jax.ShapeDtypeStruct((bd, hist, c), F32),
        ],
        compiler_params=pltpu.CompilerParams(dimension_semantics=("arbitrary",)),
        name="sample_conv",
    )(a_new, state, conv_dw, conv_dw_b)


CONV_ROWS = 64
N_CHUNKS = C_CONV // LANES
PITCH = N_CHUNKS + 1


def _hist_store(hist_ref, t0, rows, value):
    for c in range(N_CHUNKS):
        hist_ref[pl.ds(t0 * PITCH + c, rows, stride=PITCH), :] = value[:, c * LANES:(c + 1) * LANES]


def _conv_tile(hist_ref, dw_ref, dwb_ref, y_ref, tm):
    lead = HALO - (CONV_W - 1)
    groups = CONV_ROWS // SUBLANES

    def body(rb, carry):
        r0 = pl.multiple_of(rb * CONV_ROWS, CONV_ROWS)
        for c in range(N_CHUNKS):
            lanes = slice(c * LANES, (c + 1) * LANES)
            taps = [jnp.broadcast_to(dw_ref[j:j + 1, lanes], (SUBLANES, LANES)) for j in range(CONV_W)]
            accs = [jnp.broadcast_to(dwb_ref[:, lanes], (SUBLANES, LANES))] * groups
            for o in range(CONV_ROWS + CONV_W - SUBLANES):
                win = hist_ref[pl.ds((r0 + lead + o) * PITCH + c, SUBLANES, stride=PITCH), :]
                for k in range(groups):
                    j = o - SUBLANES * k
                    if 0 <= j < CONV_W:
                        accs[k] = accs[k] + taps[j] * win
            for k in range(groups):
                y_ref[pl.ds(pl.multiple_of(r0 + SUBLANES * k, SUBLANES), SUBLANES), lanes] = accs[k]
        return carry

    lax.fori_loop(0, tm // CONV_ROWS, body, 0)


LN_ROWS = 64


def _ln_swish(y_ref, act_ref, lng, lnb):
    for r0 in range(0, y_ref.shape[0], LN_ROWS):
        y = y_ref[r0:r0 + LN_ROWS, :]
        d = y - jnp.mean(y, axis=-1, keepdims=True)
        var = jnp.mean(d * d, axis=-1, keepdims=True)
        yn = d * lax.rsqrt(var + EPS) * lng + lnb
        act_ref[r0:r0 + LN_ROWS, :] = (yn * jax.nn.sigmoid(yn)).astype(BF16)


def _mix_project(act, x, attn, gates, wco, bco, wo, wout):
    attn_p = _dot(attn, wo)
    conv_o = _dot(act, wco) + bco
    mixed = (gates[:, 0:D_MODEL].astype(F32) * attn_p
             + gates[:, D_MODEL:2 * D_MODEL].astype(F32) * conv_o)
    return x + _dot(mixed.astype(BF16), wout)


def _mix_prompt_kernel(x_ref, a_ref, halo_ref, attn_ref, g_ref, dw_ref, dwb_ref, lng_ref, lnb_ref,
                       wco_ref, bco_ref, wo_ref, wout_ref, h_ref, hist_ref, y_ref, act_ref):
    tm = x_ref.shape[0]
    seq_start = pl.program_id(1) == 0
    _hist_store(hist_ref, 0, HALO, jnp.where(seq_start, 0.0, halo_ref[...]))
    _hist_store(hist_ref, HALO, tm, a_ref[...])
    _conv_tile(hist_ref, dw_ref, dwb_ref, y_ref, tm)
    _ln_swish(y_ref, act_ref, lng_ref[...], lnb_ref[...])
    h_ref[...] = _mix_project(act_ref[...], x_ref[...], attn_ref[...], g_ref[...],
                              wco_ref[...], bco_ref[...], wo_ref[...], wout_ref[...])


def _mix_sample_kernel(x_ref, y_ref, attn_ref, g_ref, lng_ref, lnb_ref,
                       wco_ref, bco_ref, wo_ref, wout_ref, h_ref, act_ref):
    _ln_swish(y_ref, act_ref, lng_ref[...], lnb_ref[...])
    h_ref[...] = _mix_project(act_ref[...], x_ref[...], attn_ref[...], g_ref[...],
                              wco_ref[...], bco_ref[...], wo_ref[...], wout_ref[...])


def _mix_prompt(x2d, a, attn, gates, batch, seq, tm, dw, dwb, lng, lnb, wco, bco, wo, wout):
    nt = seq // tm
    row = lambda b, i: (b * nt + i, 0)
    halo = lambda b, i: (b * (seq // HALO) + jnp.maximum(i * (tm // HALO) - 1, 0), 0)
    sq = (D_MODEL, D_MODEL)
    return pl.pallas_call(
        _mix_prompt_kernel,
        grid=(batch, nt),
        in_specs=[
            pl.BlockSpec((tm, D_MODEL), row),
            pl.BlockSpec((tm, C_CONV), row),
            pl.BlockSpec((HALO, C_CONV), halo),
            pl.BlockSpec((tm, Q_W), row),
            pl.BlockSpec((tm, 2 * D_MODEL), row),
            _const_spec((CONV_W, C_CONV)),
            _const_spec((1, C_CONV)),
            _const_spec((1, C_CONV)),
            _const_spec((1, C_CONV)),
            _const_spec(sq),
            _const_spec((1, D_MODEL)),
            _const_spec(sq),
            _const_spec(sq),
        ],
        out_specs=pl.BlockSpec((tm, D_MODEL), row),
        out_shape=jax.ShapeDtypeStruct((batch * seq, D_MODEL), F32),
        scratch_shapes=[pltpu.VMEM(((HALO + tm) * PITCH, LANES), F32), pltpu.VMEM((tm, C_CONV), F32),
                        pltpu.VMEM((tm, C_CONV), BF16)],
        compiler_params=pltpu.CompilerParams(dimension_semantics=("arbitrary", "arbitrary"),
                                             vmem_limit_bytes=VMEM_LIMIT),
        name="mix_prompt",
    )(x2d, a, a, attn, gates, dw, dwb, lng, lnb, wco, bco, wo, wout)


def _mix_sample(x2d, y, attn, gates, lng, lnb, wco, bco, wo, wout):
    n = x2d.shape[0]
    sq = (D_MODEL, D_MODEL)
    full = lambda w: pl.BlockSpec((n, w), lambda i: (0, 0))
    return pl.pallas_call(
        _mix_sample_kernel,
        grid=(1,),
        in_specs=[full(D_MODEL), full(C_CONV), full(Q_W), full(2 * D_MODEL),
                  _const_spec((1, C_CONV)), _const_spec((1, C_CONV)),
                  _const_spec(sq), _const_spec((1, D_MODEL)), _const_spec(sq), _const_spec(sq)],
        out_specs=full(D_MODEL),
        out_shape=jax.ShapeDtypeStruct((n, D_MODEL), F32),
        scratch_shapes=[pltpu.VMEM((n, C_CONV), BF16)],
        compiler_params=pltpu.CompilerParams(dimension_semantics=("arbitrary",),
                                             vmem_limit_bytes=VMEM_LIMIT),
        name="mix_sample",
    )(x2d, y, attn, gates, lng, lnb, wco, bco, wo, wout)


FF_CHUNK = 1024


def _ffn_kernel(h_ref, p_ref, ln2_ref, w1_ref, w2_ref, lnp_ref, wg_ref, wp_ref, o_ref, u_ref, acc_ref):
    h = h_ref[...]
    u_ref[...] = (h * _rms_scale(h) * ln2_ref[...]).astype(BF16)
    acc_ref[...] = h
    for c0 in range(0, D_FF, FF_CHUNK):
        hid = jnp.maximum(_dot(u_ref[...], w1_ref[:, c0:c0 + FF_CHUNK]), 0.0)
        acc_ref[...] += _dot((hid * hid).astype(BF16), w2_ref[c0:c0 + FF_CHUNK, :])
    h2 = acc_ref[...]
    un = (h2 * _rms_scale(h2) * lnp_ref[...]).astype(BF16)
    gate = jax.nn.sigmoid(_dot(un, wg_ref[...]))
    o_ref[...] = h2 + gate * _dot(p_ref[...].astype(BF16), wp_ref[...])


def _ffn(h, p2d, tm, ln2, w1, w2, lnp, wg, wp):
    n = h.shape[0]
    d_ple = p2d.shape[1]
    row = lambda i: (i, 0)
    return pl.pallas_call(
        _ffn_kernel,
        grid=(n // tm,),
        in_specs=[
            pl.BlockSpec((tm, D_MODEL), row),
            pl.BlockSpec((tm, d_ple), row),
            _const_spec((1, D_MODEL)),
            _const_spec((D_MODEL, D_FF)),
            _const_spec((D_FF, D_MODEL)),
            _const_spec((1, D_MODEL)),
            _const_spec((D_MODEL, D_MODEL)),
            _const_spec((d_ple, D_MODEL)),
        ],
        out_specs=pl.BlockSpec((tm, D_MODEL), row),
        out_shape=jax.ShapeDtypeStruct((n, D_MODEL), F32),
        scratch_shapes=[pltpu.VMEM((tm, D_MODEL), BF16), pltpu.VMEM((tm, D_MODEL), F32)],
        compiler_params=pltpu.CompilerParams(dimension_semantics=("arbitrary",),
                                             vmem_limit_bytes=VMEM_LIMIT),
        name="ffn",
    )(h, p2d, ln2, w1, w2, lnp, wg, wp)


def _rope_tables(pos):
    half = HEAD_DIM // 2
    inv = jnp.power(jnp.float32(ROPE_THETA), -jnp.arange(half, dtype=F32) / half)
    ang = pos.astype(F32)[:, None] * inv[None, :]
    cos = jnp.tile(jnp.cos(ang), (1, LANES // half))
    sign = jnp.tile(jnp.concatenate([-jnp.ones((half,), F32), jnp.ones((half,), F32)]), LANES // HEAD_DIM)
    sin = jnp.tile(jnp.sin(ang), (1, LANES // half)) * sign[None, :]
    return cos, sin


def _row(v):
    return v.reshape(1, -1)


def kernel(x_prompt, x_sample, cache_k, cache_v, state_conv, p_prompt, p_sample, ln1, w_in, b_glu, q_norm, k_norm, sinks, w_o_attn, conv_dw, conv_dw_b, conv_ln_g, conv_ln_b, w_conv_out, b_conv_out, w_out, ln2, w_ff1, w_ff2, ln_ple, w_ple_gate, w_ple):
    depth = w_in.shape[0]
    batch, seq, _ = x_prompt.shape
    bd, dseq, _ = x_sample.shape
    wb = cache_k.shape[2]
    hist = state_conv.shape[2]
    assert dseq == 1 and wb == WINDOW and hist == CONV_W - 1 and seq % WINDOW == 0

    tm = 512
    seg = (jnp.arange(MXU_N)[:, None] // HEAD_DIM == jnp.arange(MXU_N)[None, :] // HEAD_DIM).astype(BF16)
    cos_p, sin_p = _rope_tables(jnp.arange(seq, dtype=jnp.int32))
    cos_s, sin_s = _rope_tables(jnp.full((bd,), PAST_LEN, dtype=jnp.int32))
    head_ids = jnp.arange(N_HEADS)

    xp = x_prompt.reshape(batch * seq, D_MODEL)
    xs = x_sample.reshape(bd, D_MODEL)
    outs = [[] for _ in range(6)]
    for i in range(depth):
        win = w_in[i].astype(BF16)
        qg = _row(jnp.tile(q_norm[i], N_HEADS))
        kg = _row(jnp.tile(k_norm[i], N_KV_HEADS))
        wco = w_conv_out[i].astype(BF16)
        wo = w_o_attn[i].astype(BF16)
        wout = w_out[i].astype(BF16)
        w1 = w_ff1[i].astype(BF16)
        w2 = w_ff2[i].astype(BF16)
        wg = w_ple_gate[i].astype(BF16)
        wp = w_ple[i].astype(BF16)
        inproj_w = (_row(ln1[i]), win, _row(b_glu[i]), qg, kg, seg)
        post_w = (_row(conv_ln_g[i]), _row(conv_ln_b[i]), wco, _row(b_conv_out[i]), wo, wout)
        ffn_w = (_row(ln2[i]), w1, w2, _row(ln_ple[i]), wg, wp)

        q, k, v, kd, vd, a, g = _inproj(xp, cos_p, sin_p, seq // tm, tm, *inproj_w)
        attn = _attention(q, kd, vd, sinks[i], batch, seq)
        h = _mix_prompt(xp, a, attn, g, batch, seq, tm, conv_dw[i], _row(conv_dw_b[i]), *post_w)
        xp = _ffn(h, p_prompt[i].reshape(batch * seq, -1), tm, *ffn_w)
        outs[0].append(k.reshape(batch, seq, N_KV_HEADS, HEAD_DIM)[:, seq - WINDOW:])
        outs[1].append(v.reshape(batch, seq, N_KV_HEADS, HEAD_DIM)[:, seq - WINDOW:])
        outs[2].append(a.reshape(batch, seq, C_CONV)[:, seq - hist:])

        qs, ks, vs, _, _, a_s, gs = _inproj(xs, cos_s, sin_s, 1, bd, *inproj_w)
        qb = (jnp.zeros((bd, N_HEADS, N_KV_HEADS, HEAD_DIM), BF16)
              .at[:, head_ids, head_ids // GROUP].set(qs.reshape(bd, N_HEADS, HEAD_DIM))
              .reshape(bd, N_HEADS, KV_W))
        nk, nv, ao = _sample_attention(qb, ks.reshape(bd, 1, KV_W), vs.reshape(bd, 1, KV_W),
                                       cache_k[i].reshape(bd, wb, KV_W), cache_v[i].reshape(bd, wb, KV_W),
                                       sinks[i].reshape(N_HEADS, 1), 8)
        attn_s = (ao.reshape(bd, N_HEADS, N_KV_HEADS, HEAD_DIM)[:, head_ids, head_ids // GROUP]
                  .reshape(bd, Q_W).astype(BF16))
        y_s, ns = _sample_conv(a_s.reshape(bd, 1, C_CONV), state_conv[i], conv_dw[i], _row(conv_dw_b[i]), 8)
        hs = _mix_sample(xs, y_s.reshape(bd, C_CONV), attn_s, gs, *post_w)
        xs = _ffn(hs, p_sample[i].reshape(bd, -1), bd, *ffn_w)
        outs[3].append(nk.reshape(bd, wb, N_KV_HEADS, HEAD_DIM))
        outs[4].append(nv.reshape(bd, wb, N_KV_HEADS, HEAD_DIM))
        outs[5].append(ns)

    return (xp.reshape(batch, seq, D_MODEL), xs.reshape(bd, dseq, D_MODEL),
            jnp.stack(outs[0]), jnp.stack(outs[1]), jnp.stack(outs[2]),
            jnp.stack(outs[3]), jnp.stack(outs[4]), jnp.stack(outs[5]))
```

```python
import functools

import jax
import jax.numpy as jnp
from jax import lax
from jax.experimental import pallas as pl
from jax.experimental.pallas import tpu as pltpu

D_MODEL = 1024
N_HEADS = 16
N_KV_HEADS = 4
HEAD_DIM = 64
GROUP = N_HEADS // N_KV_HEADS
WINDOW = 128
ROPE_THETA = 10000.0
PAST_LEN = 16384
C_CONV = D_MODEL
CONV_W = 31
D_FF = 4 * D_MODEL
EPS = 1e-6
NEG = -1e30
Q_W = N_HEADS * HEAD_DIM
KV_W = N_KV_HEADS * HEAD_DIM
D_IN = Q_W + 2 * KV_W + 2 * C_CONV + 2 * D_MODEL

LOG2E = 1.4426950408889634
Q_SCALE = HEAD_DIM ** -0.5 * LOG2E

LANES = 128
SUBLANES = 8
BF16_ROWS = 16
MXU_N = 256
HALO = 32
VMEM_LIMIT = 56 * 1024 * 1024

F32 = jnp.float32
BF16 = jnp.bfloat16


def _dot(a, b):
    return jnp.dot(a, b, preferred_element_type=F32)


def _dot_nt(a, b):
    return lax.dot_general(a, b, (((1,), (1,)), ((), ())), preferred_element_type=F32)


def _const_spec(shape):
    nd = len(shape)
    return pl.BlockSpec(shape, lambda *_: (0,) * nd, pipeline_mode=pl.Buffered(1))


def _rms_scale(x):
    return lax.rsqrt(jnp.mean(x * x, axis=-1, keepdims=True) + EPS)


def _head_norm_rope(z, gain, seg, cos, sin_signed, scale):
    tm, width = z.shape
    outs = []
    lane = lax.broadcasted_iota(jnp.int32, (tm, LANES), 1)
    first_half = (lane & (HEAD_DIM - 1)) < (HEAD_DIM // 2)
    for c0 in range(0, width, MXU_N):
        zc = z[:, c0:c0 + MXU_N]
        ss = _dot((zc * zc).astype(BF16), seg)
        zn = zc * lax.rsqrt(ss * (1.0 / HEAD_DIM) + EPS) * gain[:, c0:c0 + MXU_N]
        for l0 in range(0, MXU_N, LANES):
            xc = zn[:, l0:l0 + LANES]
            partner = jnp.where(first_half,
                                pltpu.roll(xc, LANES - HEAD_DIM // 2, axis=1),
                                pltpu.roll(xc, HEAD_DIM // 2, axis=1))
            outs.append((xc * cos + partner * sin_signed) * scale)
    return outs


def _store_head_pairs(dup_ref, i, xc):
    low = lax.broadcasted_iota(jnp.int32, xc.shape, 1) < HEAD_DIM
    rolled = pltpu.roll(xc, HEAD_DIM, axis=1)
    dup_ref[:, (2 * i) * LANES:(2 * i + 1) * LANES] = jnp.where(low, xc, rolled).astype(BF16)
    dup_ref[:, (2 * i + 1) * LANES:(2 * i + 2) * LANES] = jnp.where(low, rolled, xc).astype(BF16)


def _inproj_kernel(x_ref, ln1_ref, w_ref, bglu_ref, qg_ref, kg_ref, cos_ref, sin_ref, seg_ref,
                   q_ref, k_ref, v_ref, kd_ref, vd_ref, a_ref, g_ref, u_ref):
    x = x_ref[...]
    u_ref[...] = (x * _rms_scale(x) * ln1_ref[...]).astype(BF16)
    cos = cos_ref[...]
    sin = sin_ref[...]
    seg = seg_ref[...]

    zq = _dot(u_ref[...], w_ref[:, 0:Q_W])
    for i, qc in enumerate(_head_norm_rope(zq, qg_ref[...], seg, cos, sin, Q_SCALE)):
        q_ref[:, i * LANES:(i + 1) * LANES] = qc.astype(BF16)

    zkv = _dot(u_ref[...], w_ref[:, Q_W:Q_W + 2 * KV_W])
    zv = zkv[:, KV_W:2 * KV_W]
    for i, kc in enumerate(_head_norm_rope(zkv[:, 0:KV_W], kg_ref[...], seg, cos, sin, 1.0)):
        k_ref[:, i * LANES:(i + 1) * LANES] = kc
        _store_head_pairs(kd_ref, i, kc)
    v_ref[...] = zv
    for i in range(KV_W // LANES):
        _store_head_pairs(vd_ref, i, zv[:, i * LANES:(i + 1) * LANES])

    glu0 = Q_W + 2 * KV_W
    lin = _dot(u_ref[...], w_ref[:, glu0:glu0 + C_CONV]) + bglu_ref[:, 0:C_CONV]
    gate = _dot(u_ref[...], w_ref[:, glu0 + C_CONV:glu0 + 2 * C_CONV]) + bglu_ref[:, C_CONV:2 * C_CONV]
    a_ref[...] = lin * jax.nn.sigmoid(gate)

    g0 = glu0 + 2 * C_CONV
    for c0 in range(0, 2 * D_MODEL, D_MODEL):
        g_ref[:, c0:c0 + D_MODEL] = jax.nn.sigmoid(_dot(u_ref[...], w_ref[:, g0 + c0:g0 + c0 + D_MODEL])).astype(BF16)


def _inproj(x2d, cos, sin, pos_tiles, tm, ln1, w_in, b_glu, qg, kg, seg):
    n = x2d.shape[0]
    grid = (n // tm,)
    row = lambda i: (i, 0)
    pos = lambda i: (i % pos_tiles, 0)
    return pl.pallas_call(
        _inproj_kernel,
        grid=grid,
        in_specs=[
            pl.BlockSpec((tm, D_MODEL), row),
            _const_spec((1, D_MODEL)),
            _const_spec((D_MODEL, D_IN)),
            _const_spec((1, 2 * C_CONV)),
            _const_spec((1, Q_W)),
            _const_spec((1, KV_W)),
            pl.BlockSpec((tm, LANES), pos),
            pl.BlockSpec((tm, LANES), pos),
            _const_spec((MXU_N, MXU_N)),
        ],
        out_specs=[
            pl.BlockSpec((tm, Q_W), row),
            pl.BlockSpec((tm, KV_W), row),
            pl.BlockSpec((tm, KV_W), row),
            pl.BlockSpec((tm, 2 * KV_W), row),
            pl.BlockSpec((tm, 2 * KV_W), row),
            pl.BlockSpec((tm, C_CONV), row),
            pl.BlockSpec((tm, 2 * D_MODEL), row),
        ],
        out_shape=[
            jax.ShapeDtypeStruct((n, Q_W), BF16),
            jax.ShapeDtypeStruct((n, KV_W), F32),
            jax.ShapeDtypeStruct((n, KV_W), F32),
            jax.ShapeDtypeStruct((n, 2 * KV_W), BF16),
            jax.ShapeDtypeStruct((n, 2 * KV_W), BF16),
            jax.ShapeDtypeStruct((n, C_CONV), F32),
            jax.ShapeDtypeStruct((n, 2 * D_MODEL), BF16),
        ],
        scratch_shapes=[pltpu.VMEM((tm, D_MODEL), BF16)],
        compiler_params=pltpu.CompilerParams(dimension_semantics=("arbitrary",),
                                             vmem_limit_bytes=VMEM_LIMIT),
        name="in_proj",
    )(x2d, ln1, w_in, b_glu, qg, kg, cos, sin, seg)


def _attn_kernel(sinks_ref, q_ref, kp_ref, kc_ref, vp_ref, vc_ref, o_ref):
    step = pl.program_id(1)
    bq = WINDOW
    nk = 2 * bq
    qrow = lax.broadcasted_iota(jnp.int32, (bq, nk), 0)
    col = lax.broadcasted_iota(jnp.int32, (bq, nk), 1)
    band = (col > qrow) & (col <= qrow + WINDOW)
    base_first = jnp.where(band & ((col >= bq) | (step > 0)), 0.0, NEG)
    base_rest = jnp.where(band, 0.0, NEG)
    lane = lax.broadcasted_iota(jnp.int32, (bq, LANES), 1)
    sink_slot = lane == 0
    low = lane < HEAD_DIM
    first_row = lax.broadcasted_iota(jnp.int32, (BF16_ROWS, LANES), 0) == 0
    ones = jnp.ones((nk, LANES), BF16)

    def window(prev_ref, p0, cur_ref, c0, lanes):
        head = prev_ref[p0:p0 + BF16_ROWS, lanes]
        head = jnp.where(first_row, jnp.zeros_like(head), head)
        return jnp.concatenate([head, prev_ref[p0 + BF16_ROWS:p0 + bq, lanes], cur_ref[c0:c0 + bq, lanes]],
                               axis=0)

    for sub in range(q_ref.shape[0] // bq):
        rows = slice(sub * bq, (sub + 1) * bq)
        base = base_first if sub == 0 else base_rest
        k_prev, v_prev, p0 = (kp_ref, vp_ref, 0) if sub == 0 else (kc_ref, vc_ref, (sub - 1) * bq)
        for h in range(N_KV_HEADS):
            lanes = slice(h * LANES, (h + 1) * LANES)
            kmat = window(k_prev, p0, kc_ref, sub * bq, lanes)
            vmat = window(v_prev, p0, vc_ref, sub * bq, lanes)
            vrhs = jnp.concatenate([vmat, ones], axis=1)
            q_parts = []
            bias_parts = []
            for g in range(GROUP):
                qc = q_ref[rows, (2 * h + g // 2) * LANES:(2 * h + g // 2 + 1) * LANES]
                zero = jnp.zeros_like(qc)
                q_parts.append(jnp.where(low, qc, zero) if g % 2 == 0 else jnp.where(low, zero, qc))
                sink = sinks_ref[GROUP * h + g] * LOG2E
                bias_parts.append(jnp.concatenate(
                    [jnp.where(sink_slot, sink, base[:, 0:LANES]), base[:, LANES:]], axis=1))
            s = _dot_nt(jnp.concatenate(q_parts, axis=0), kmat) + jnp.concatenate(bias_parts, axis=0)
            m = jnp.max(s, axis=-1, keepdims=True)
            e = jnp.exp2(s - m).astype(BF16)
            pvd = _dot(e, vrhs)
            o = pvd[:, 0:LANES] / pvd[:, LANES:2 * LANES]
            for half in range(2):
                r0 = 2 * half * bq
                o_ref[rows, (2 * h + half) * LANES:(2 * h + half + 1) * LANES] = jnp.where(
                    low, o[r0:r0 + bq], o[r0 + bq:r0 + 2 * bq]).astype(BF16)


ATTN_BLOCKS = 4


def _attention(q, kd, vd, sinks, batch, seq):
    bq = WINDOW
    kvw = N_KV_HEADS * LANES
    nb = seq // bq
    ns = nb // ATTN_BLOCKS
    cur = lambda b, i: (b * ns + i, 0)
    prev = lambda b, i: (b * nb + jnp.maximum(i * ATTN_BLOCKS - 1, 0), 0)
    return pl.pallas_call(
        _attn_kernel,
        grid=(batch, ns),
        in_specs=[
            pl.BlockSpec(memory_space=pltpu.SMEM),
            pl.BlockSpec((ATTN_BLOCKS * bq, Q_W), cur),
            pl.BlockSpec((bq, kvw), prev),
            pl.BlockSpec((ATTN_BLOCKS * bq, kvw), cur),
            pl.BlockSpec((bq, kvw), prev),
            pl.BlockSpec((ATTN_BLOCKS * bq, kvw), cur),
        ],
        out_specs=pl.BlockSpec((ATTN_BLOCKS * bq, Q_W), cur),
        out_shape=jax.ShapeDtypeStruct((batch * seq, Q_W), BF16),
        compiler_params=pltpu.CompilerParams(dimension_semantics=("arbitrary", "arbitrary")),
        name="swa_attention",
    )(sinks, q, kd, kd, vd, vd)


def _sample_attn_kernel(sinks_ref, q_ref, knew_ref, vnew_ref, ck_ref, cv_ref,
                        nk_ref, nv_ref, o_ref):
    nb, wb, nkv, hd = ck_ref.shape
    rows = wb * nkv
    sink = sinks_ref[...] * LOG2E
    head = lax.broadcasted_iota(jnp.int32, (N_HEADS, rows), 0)
    col = lax.broadcasted_iota(jnp.int32, (N_HEADS, rows), 1)
    own_head = (head >> (GROUP.bit_length() - 1)) == (col & (nkv - 1))
    oldest = lax.broadcasted_iota(jnp.int32, (SUBLANES, hd), 0) < nkv
    for b in range(nb):
        nk_ref[b, 0:wb - 1] = ck_ref[b, 1:wb]
        nk_ref[b, wb - 1] = knew_ref[b, 0:nkv, :]
        nv_ref[b, 0:wb - 1] = cv_ref[b, 1:wb]
        nv_ref[b, wb - 1] = vnew_ref[b, 0:nkv, :]
        keys = ck_ref[b].reshape(rows, hd)
        vals = cv_ref[b].reshape(rows, hd)
        keys = jnp.concatenate([jnp.where(oldest, knew_ref[b], keys[0:SUBLANES]), keys[SUBLANES:]],
                               axis=0).astype(BF16)
        vals = jnp.concatenate([jnp.where(oldest, vnew_ref[b], vals[0:SUBLANES]), vals[SUBLANES:]],
                               axis=0).astype(BF16)
        s = jnp.where(own_head, _dot_nt(q_ref[b], keys), NEG)
        m = jnp.maximum(jnp.max(s, axis=-1, keepdims=True), sink)
        e = jnp.exp2(s - m)
        den = jnp.sum(e, axis=-1, keepdims=True) + jnp.exp2(sink - m)
        o_ref[b] = _dot(e.astype(BF16), vals) / den


def _sample_attention(q3, k_new, v_new, cache_k, cache_v, sinks_col, nb):
    bd, wb, nkv, hd = cache_k.shape
    blk3 = lambda i: (i, 0, 0)
    blk4 = lambda i: (i, 0, 0, 0)
    return pl.pallas_call(
        _sample_attn_kernel,
        grid=(bd // nb,),
        in_specs=[
            _const_spec((N_HEADS, 1)),
            pl.BlockSpec((nb, N_HEADS, hd), blk3),
            pl.BlockSpec((nb, SUBLANES, hd), blk3),
            pl.BlockSpec((nb, SUBLANES, hd), blk3),
            pl.BlockSpec((nb, wb, nkv, hd), blk4),
            pl.BlockSpec((nb, wb, nkv, hd), blk4),
        ],
        out_specs=[
            pl.BlockSpec((nb, wb, nkv, hd), blk4),
            pl.BlockSpec((nb, wb, nkv, hd), blk4),
            pl.BlockSpec((nb, N_HEADS, hd), blk3),
        ],
        out_shape=[
            jax.ShapeDtypeStruct((bd, wb, nkv, hd), F32),
            jax.ShapeDtypeStruct((bd, wb, nkv, hd), F32),
            jax.ShapeDtypeStruct((bd, N_HEADS, hd), F32),
        ],
        compiler_params=pltpu.CompilerParams(dimension_semantics=("arbitrary",),
                                             vmem_limit_bytes=VMEM_LIMIT),
        name="sample_attention",
    )(sinks_col, q3, k_new, v_new, cache_k, cache_v)


def _sample_conv_kernel(a_ref, st_ref, dw_ref, dwb_ref, y_ref, ns_ref):
    nb, hist, _ = st_ref.shape
    w_hist = dw_ref[0:hist, :]
    w_last = dw_ref[hist:hist + 1, :]
    for b in range(nb):
        a_new = a_ref[b]
        y_ref[b] = (jnp.sum(st_ref[b] * w_hist, axis=0, keepdims=True)
                    + a_new * w_last + dwb_ref[...])
        ns_ref[b, 0:hist - 1, :] = st_ref[b, 1:hist, :]
        ns_ref[b, hist - 1:hist, :] = a_new


def _sample_conv(a_new, state, conv_dw, conv_dw_b, nb):
    bd, hist, c = state.shape
    blk3 = lambda i: (i, 0, 0)
    return pl.pallas_call(
        _sample_conv_kernel,
        grid=(bd // nb,),
        in_specs=[
            pl.BlockSpec((nb, 1, c), blk3),
            pl.BlockSpec((nb, hist, c), blk3),
            _const_spec((CONV_W, c)),
            _const_spec((1, c)),
        ],
        out_specs=[
            pl.BlockSpec((nb, 1, c), blk3),
            pl.BlockSpec((nb, hist, c), blk3),
        ],
        out_shape=[
            jax.ShapeDtypeStruct((bd, 1, c), F32),
            jax.ShapeDtypeStruct((bd, hist, c), F32),
        ],
        compiler_params=pltpu.CompilerParams(dimension_semantics=("arbitrary",)),
        name="sample_conv",
    )(a_new, state, conv_dw, conv_dw_b)


CONV_ROWS = 64
N_CHUNKS = C_CONV // LANES
PITCH = N_CHUNKS + 1


def _hist_store(hist_ref, t0, rows, value):
    for c in range(N_CHUNKS):
        hist_ref[pl.ds(t0 * PITCH + c, rows, stride=PITCH), :] = value[:, c * LANES:(c + 1) * LANES]


def _conv_tile(hist_ref, dw_ref, dwb_ref, y_ref, tm):
    lead = HALO - (CONV_W - 1)
    groups = CONV_ROWS // SUBLANES

    def body(rb, carry):
        r0 = pl.multiple_of(rb * CONV_ROWS, CONV_ROWS)
        for c in range(N_CHUNKS):
            lanes = slice(c * LANES, (c + 1) * LANES)
            taps = [jnp.broadcast_to(dw_ref[j:j + 1, lanes], (SUBLANES, LANES)) for j in range(CONV_W)]
            accs = [jnp.broadcast_to(dwb_ref[:, lanes], (SUBLANES, LANES))] * groups
            for o in range(CONV_ROWS + CONV_W - SUBLANES):
                win = hist_ref[pl.ds((r0 + lead + o) * PITCH + c, SUBLANES, stride=PITCH), :]
                for k in range(groups):
                    j = o - SUBLANES * k
                    if 0 <= j < CONV_W:
                        accs[k] = accs[k] + taps[j] * win
            for k in range(groups):
                y_ref[pl.ds(pl.multiple_of(r0 + SUBLANES * k, SUBLANES), SUBLANES), lanes] = accs[k]
        return carry

    lax.fori_loop(0, tm // CONV_ROWS, body, 0)


LN_ROWS = 64


def _ln_swish(y_ref, act_ref, lng, lnb):
    for r0 in range(0, y_ref.shape[0], LN_ROWS):
        y = y_ref[r0:r0 + LN_ROWS, :]
        d = y - jnp.mean(y, axis=-1, keepdims=True)
        var = jnp.mean(d * d, axis=-1, keepdims=True)
        yn = d * lax.rsqrt(var + EPS) * lng + lnb
        act_ref[r0:r0 + LN_ROWS, :] = (yn * jax.nn.sigmoid(yn)).astype(BF16)


def _mix_project(act, x, attn, gates, wco, bco, wo, wout):
    attn_p = _dot(attn, wo)
    conv_o = _dot(act, wco) + bco
    mixed = (gates[:, 0:D_MODEL].astype(F32) * attn_p
             + gates[:, D_MODEL:2 * D_MODEL].astype(F32) * conv_o)
    return x + _dot(mixed.astype(BF16), wout)


def _mix_prompt_kernel(x_ref, a_ref, halo_ref, attn_ref, g_ref, dw_ref, dwb_ref, lng_ref, lnb_ref,
                       wco_ref, bco_ref, wo_ref, wout_ref, h_ref, hist_ref, y_ref, act_ref):
    tm = x_ref.shape[0]
    seq_start = pl.program_id(1) == 0
    _hist_store(hist_ref, 0, HALO, jnp.where(seq_start, 0.0, halo_ref[...]))
    _hist_store(hist_ref, HALO, tm, a_ref[...])
    _conv_tile(hist_ref, dw_ref, dwb_ref, y_ref, tm)
    _ln_swish(y_ref, act_ref, lng_ref[...], lnb_ref[...])
    h_ref[...] = _mix_project(act_ref[...], x_ref[...], attn_ref[...], g_ref[...],
                              wco_ref[...], bco_ref[...], wo_ref[...], wout_ref[...])


def _mix_sample_kernel(x_ref, y_ref, attn_ref, g_ref, lng_ref, lnb_ref,
                       wco_ref, bco_ref, wo_ref, wout_ref, h_ref, act_ref):
    _ln_swish(y_ref, act_ref, lng_ref[...], lnb_ref[...])
    h_ref[...] = _mix_project(act_ref[...], x_ref[...], attn_ref[...], g_ref[...],
                              wco_ref[...], bco_ref[...], wo_ref[...], wout_ref[...])


def _mix_prompt(x2d, a, attn, gates, batch, seq, tm, dw, dwb, lng, lnb, wco, bco, wo, wout):
    nt = seq // tm
    row = lambda b, i: (b * nt + i, 0)
    halo = lambda b, i: (b * (seq // HALO) + jnp.maximum(i * (tm // HALO) - 1, 0), 0)
    sq = (D_MODEL, D_MODEL)
    return pl.pallas_call(
        _mix_prompt_kernel,
        grid=(batch, nt),
        in_specs=[
            pl.BlockSpec((tm, D_MODEL), row),
            pl.BlockSpec((tm, C_CONV), row),
            pl.BlockSpec((HALO, C_CONV), halo),
            pl.BlockSpec((tm, Q_W), row),
            pl.BlockSpec((tm, 2 * D_MODEL), row),
            _const_spec((CONV_W, C_CONV)),
            _const_spec((1, C_CONV)),
            _const_spec((1, C_CONV)),
            _const_spec((1, C_CONV)),
            _const_spec(sq),
            _const_spec((1, D_MODEL)),
            _const_spec(sq),
            _const_spec(sq),
        ],
        out_specs=pl.BlockSpec((tm, D_MODEL), row),
        out_shape=jax.ShapeDtypeStruct((batch * seq, D_MODEL), F32),
        scratch_shapes=[pltpu.VMEM(((HALO + tm) * PITCH, LANES), F32), pltpu.VMEM((tm, C_CONV), F32),
                        pltpu.VMEM((tm, C_CONV), BF16)],
        compiler_params=pltpu.CompilerParams(dimension_semantics=("arbitrary", "arbitrary"),
                                             vmem_limit_bytes=VMEM_LIMIT),
        name="mix_prompt",
    )(x2d, a, a, attn, gates, dw, dwb, lng, lnb, wco, bco, wo, wout)


def _mix_sample(x2d, y, attn, gates, lng, lnb, wco, bco, wo, wout):
    n = x2d.shape[0]
    sq = (D_MODEL, D_MODEL)
    full = lambda w: pl.BlockSpec((n, w), lambda i: (0, 0))
    return pl.pallas_call(
        _mix_sample_kernel,
        grid=(1,),
        in_specs=[full(D_MODEL), full(C_CONV), full(Q_W), full(2 * D_MODEL),
                  _const_spec((1, C_CONV)), _const_spec((1, C_CONV)),
                  _const_spec(sq), _const_spec((1, D_MODEL)), _const_spec(sq), _const_spec(sq)],
        out_specs=full(D_MODEL),
        out_shape=jax.ShapeDtypeStruct((n, D_MODEL), F32),
        scratch_shapes=[pltpu.VMEM((n, C_CONV), BF16)],
        compiler_params=pltpu.CompilerParams(dimension_semantics=("arbitrary",),
                                             vmem_limit_bytes=VMEM_LIMIT),
        name="mix_sample",
    )(x2d, y, attn, gates, lng, lnb, wco, bco, wo, wout)


FF_CHUNK = 1024


def _ffn_kernel(h_ref, p_ref, ln2_ref, w1_ref, w2_ref, lnp_ref, wg_ref, wp_ref, o_ref, u_ref, acc_ref):
    h = h_ref[...]
    u_ref[...] = (h * _rms_scale(h) * ln2_ref[...]).astype(BF16)
    acc_ref[...] = h
    for c0 in range(0, D_FF, FF_CHUNK):
        hid = jnp.maximum(_dot(u_ref[...], w1_ref[:, c0:c0 + FF_CHUNK]), 0.0)
        acc_ref[...] += _dot((hid * hid).astype(BF16), w2_ref[c0:c0 + FF_CHUNK, :])
    h2 = acc_ref[...]
    un = (h2 * _rms_scale(h2) * lnp_ref[...]).astype(BF16)
    gate = jax.nn.sigmoid(_dot(un, wg_ref[...]))
    o_ref[...] = h2 + gate * _dot(p_ref[...].astype(BF16), wp_ref[...])


def _ffn(h, p2d, tm, ln2, w1, w2, lnp, wg, wp):
    n = h.shape[0]
    d_ple = p2d.shape[1]
    row = lambda i: (i, 0)
    return pl.pallas_call(
        _ffn_kernel,
        grid=(n // tm,),
        in_specs=[
            pl.BlockSpec((tm, D_MODEL), row),
            pl.BlockSpec((tm, d_ple), row),
            _const_spec((1, D_MODEL)),
            _const_spec((D_MODEL, D_FF)),
            _const_spec((D_FF, D_MODEL)),
            _const_spec((1, D_MODEL)),
            _const_spec((D_MODEL, D_MODEL)),
            _const_spec((d_ple, D_MODEL)),
        ],
        out_specs=pl.BlockSpec((tm, D_MODEL), row),
        out_shape=jax.ShapeDtypeStruct((n, D_MODEL), F32),
        scratch_shapes=[pltpu.VMEM((tm, D_MODEL), BF16), pltpu.VMEM((tm, D_MODEL), F32)],
        compiler_params=pltpu.CompilerParams(dimension_semantics=("arbitrary",),
                                             vmem_limit_bytes=VMEM_LIMIT),
        name="ffn",
    )(h, p2d, ln2, w1, w2, lnp, wg, wp)


def _rope_tables(pos):
    half = HEAD_DIM // 2
    inv = jnp.power(jnp.float32(ROPE_THETA), -jnp.arange(half, dtype=F32) / half)
    ang = pos.astype(F32)[:, None] * inv[None, :]
    cos = jnp.tile(jnp.cos(ang), (1, LANES // half))
    sign = jnp.tile(jnp.concatenate([-jnp.ones((half,), F32), jnp.ones((half,), F32)]), LANES // HEAD_DIM)
    sin = jnp.tile(jnp.sin(ang), (1, LANES // half)) * sign[None, :]
    return cos, sin


def _row(v):
    return v.reshape(1, -1)


def kernel(x_prompt, x_sample, cache_k, cache_v, state_conv, p_prompt, p_sample, ln1, w_in, b_glu, q_norm, k_norm, sinks, w_o_attn, conv_dw, conv_dw_b, conv_ln_g, conv_ln_b, w_conv_out, b_conv_out, w_out, ln2, w_ff1, w_ff2, ln_ple, w_ple_gate, w_ple):
    depth = w_in.shape[0]
    batch, seq, _ = x_prompt.shape
    bd, dseq, _ = x_sample.shape
    wb = cache_k.shape[2]
    hist = state_conv.shape[2]
    assert dseq == 1 and wb == WINDOW and hist == CONV_W - 1 and seq % WINDOW == 0

    tm = 512
    seg = (jnp.arange(MXU_N)[:, None] // HEAD_DIM == jnp.arange(MXU_N)[None, :] // HEAD_DIM).astype(BF16)
    cos_p, sin_p = _rope_tables(jnp.arange(seq, dtype=jnp.int32))
    cos_s, sin_s = _rope_tables(jnp.full((bd,), PAST_LEN, dtype=jnp.int32))

    xp = x_prompt.reshape(batch * seq, D_MODEL)
    xs = x_sample.reshape(bd, D_MODEL)
    outs = [[] for _ in range(6)]
    for i in range(depth):
        win = w_in[i].astype(BF16)
        qg = _row(jnp.tile(q_norm[i], N_HEADS))
        kg = _row(jnp.tile(k_norm[i], N_KV_HEADS))
        wco = w_conv_out[i].astype(BF16)
        wo = w_o_attn[i].astype(BF16)
        wout = w_out[i].astype(BF16)
        w1 = w_ff1[i].astype(BF16)
        w2 = w_ff2[i].astype(BF16)
        wg = w_ple_gate[i].astype(BF16)
        wp = w_ple[i].astype(BF16)
        inproj_w = (_row(ln1[i]), win, _row(b_glu[i]), qg, kg, seg)
        post_w = (_row(conv_ln_g[i]), _row(conv_ln_b[i]), wco, _row(b_conv_out[i]), wo, wout)
        ffn_w = (_row(ln2[i]), w1, w2, _row(ln_ple[i]), wg, wp)

        q, k, v, kd, vd, a, g = _inproj(xp, cos_p, sin_p, seq // tm, tm, *inproj_w)
        attn = _attention(q, kd, vd, sinks[i], batch, seq)
        h = _mix_prompt(xp, a, attn, g, batch, seq, tm, conv_dw[i], _row(conv_dw_b[i]), *post_w)
        xp = _ffn(h, p_prompt[i].reshape(batch * seq, -1), tm, *ffn_w)
        outs[0].append(k.reshape(batch, seq, N_KV_HEADS, HEAD_DIM)[:, seq - WINDOW:])
        outs[1].append(v.reshape(batch, seq, N_KV_HEADS, HEAD_DIM)[:, seq - WINDOW:])
        outs[2].append(a.reshape(batch, seq, C_CONV)[:, seq - hist:])

        qs, ks, vs, _, _, a_s, gs = _inproj(xs, cos_s, sin_s, 1, bd, *inproj_w)
        kn = jnp.tile(ks.reshape(bd, N_KV_HEADS, HEAD_DIM), (1, SUBLANES // N_KV_HEADS, 1))
        vn = jnp.tile(vs.reshape(bd, N_KV_HEADS, HEAD_DIM), (1, SUBLANES // N_KV_HEADS, 1))
        nk, nv, ao = _sample_attention(qs.reshape(bd, N_HEADS, HEAD_DIM), kn, vn, cache_k[i], cache_v[i],
                                       sinks[i].reshape(N_HEADS, 1), 8)
        attn_s = ao.reshape(bd, Q_W).astype(BF16)
        y_s, ns = _sample_conv(a_s.reshape(bd, 1, C_CONV), state_conv[i], conv_dw[i], _row(conv_dw_b[i]), 8)
        hs = _mix_sample(xs, y_s.reshape(bd, C_CONV), attn_s, gs, *post_w)
        xs = _ffn(hs, p_sample[i].reshape(bd, -1), bd, *ffn_w)
        outs[3].append(nk)
        outs[4].append(nv)
        outs[5].append(ns)

    return (xp.reshape(batch, seq, D_MODEL), xs.reshape(bd, dseq, D_MODEL),
            jnp.stack(outs[0]), jnp.stack(outs[1]), jnp.stack(outs[2]),
            jnp.stack(outs[3]), jnp.stack(outs[4]), jnp.stack(outs[5]))
```

```python
import functools

import jax
import jax.numpy as jnp
from jax import lax
from jax.experimental import pallas as pl
from jax.experimental.pallas import tpu as pltpu

D_MODEL = 1024
N_HEADS = 16
N_KV_HEADS = 4
HEAD_DIM = 64
GROUP = N_HEADS // N_KV_HEADS
WINDOW = 128
ROPE_THETA = 10000.0
PAST_LEN = 16384
C_CONV = D_MODEL
CONV_W = 31
D_FF = 4 * D_MODEL
EPS = 1e-6
NEG = -1e30
Q_W = N_HEADS * HEAD_DIM
KV_W = N_KV_HEADS * HEAD_DIM
D_IN = Q_W + 2 * KV_W + 2 * C_CONV + 2 * D_MODEL

LOG2E = 1.4426950408889634
Q_SCALE = HEAD_DIM ** -0.5 * LOG2E

LANES = 128
SUBLANES = 8
BF16_ROWS = 16
MXU_N = 256
HALO = 32
VMEM_LIMIT = 56 * 1024 * 1024

F32 = jnp.float32
BF16 = jnp.bfloat16


def _dot(a, b):
    return jnp.dot(a, b, preferred_element_type=F32)


def _dot_nt(a, b):
    return lax.dot_general(a, b, (((1,), (1,)), ((), ())), preferred_element_type=F32)


def _const_spec(shape):
    nd = len(shape)
    return pl.BlockSpec(shape, lambda *_: (0,) * nd, pipeline_mode=pl.Buffered(1))


def _rms_scale(x):
    return lax.rsqrt(jnp.mean(x * x, axis=-1, keepdims=True) + EPS)


def _head_norm_rope(z, gain, seg, cos, sin_signed, scale):
    tm, width = z.shape
    outs = []
    lane = lax.broadcasted_iota(jnp.int32, (tm, LANES), 1)
    first_half = (lane & (HEAD_DIM - 1)) < (HEAD_DIM // 2)
    for c0 in range(0, width, MXU_N):
        zc = z[:, c0:c0 + MXU_N]
        ss = _dot((zc * zc).astype(BF16), seg)
        zn = zc * lax.rsqrt(ss * (1.0 / HEAD_DIM) + EPS) * gain[:, c0:c0 + MXU_N]
        for l0 in range(0, MXU_N, LANES):
            xc = zn[:, l0:l0 + LANES]
            partner = jnp.where(first_half,
                                pltpu.roll(xc, LANES - HEAD_DIM // 2, axis=1),
                                pltpu.roll(xc, HEAD_DIM // 2, axis=1))
            outs.append((xc * cos + partner * sin_signed) * scale)
    return outs


def _store_head_pairs(dup_ref, i, xc):
    low = lax.broadcasted_iota(jnp.int32, xc.shape, 1) < HEAD_DIM
    rolled = pltpu.roll(xc, HEAD_DIM, axis=1)
    dup_ref[:, (2 * i) * LANES:(2 * i + 1) * LANES] = jnp.where(low, xc, rolled).astype(BF16)
    dup_ref[:, (2 * i + 1) * LANES:(2 * i + 2) * LANES] = jnp.where(low, rolled, xc).astype(BF16)


def _inproj_kernel(x_ref, ln1_ref, w_ref, bglu_ref, qg_ref, kg_ref, cos_ref, sin_ref, seg_ref,
                   q_ref, k_ref, v_ref, kd_ref, vd_ref, a_ref, g_ref, u_ref):
    x = x_ref[...]
    u_ref[...] = (x * ln1_ref[...]).astype(BF16)
    rs = _rms_scale(x)
    cos = cos_ref[...]
    sin = sin_ref[...]
    seg = seg_ref[...]

    zq = _dot(u_ref[...], w_ref[:, 0:Q_W]) * rs
    for i, qc in enumerate(_head_norm_rope(zq, qg_ref[...], seg, cos, sin, Q_SCALE)):
        q_ref[:, i * LANES:(i + 1) * LANES] = qc.astype(BF16)

    zkv = _dot(u_ref[...], w_ref[:, Q_W:Q_W + 2 * KV_W]) * rs
    zv = zkv[:, KV_W:2 * KV_W]
    for i, kc in enumerate(_head_norm_rope(zkv[:, 0:KV_W], kg_ref[...], seg, cos, sin, 1.0)):
        k_ref[:, i * LANES:(i + 1) * LANES] = kc
        _store_head_pairs(kd_ref, i, kc)
    v_ref[...] = zv
    for i in range(KV_W // LANES):
        _store_head_pairs(vd_ref, i, zv[:, i * LANES:(i + 1) * LANES])

    glu0 = Q_W + 2 * KV_W
    lin = _dot(u_ref[...], w_ref[:, glu0:glu0 + C_CONV]) * rs + bglu_ref[:, 0:C_CONV]
    gate = _dot(u_ref[...], w_ref[:, glu0 + C_CONV:glu0 + 2 * C_CONV]) * rs + bglu_ref[:, C_CONV:2 * C_CONV]
    a_ref[...] = lin * jax.nn.sigmoid(gate)

    g0 = glu0 + 2 * C_CONV
    for c0 in range(0, 2 * D_MODEL, D_MODEL):
        g_ref[:, c0:c0 + D_MODEL] = jax.nn.sigmoid(
            _dot(u_ref[...], w_ref[:, g0 + c0:g0 + c0 + D_MODEL]) * rs).astype(BF16)


def _inproj(x2d, cos, sin, pos_tiles, tm, ln1, w_in, b_glu, qg, kg, seg):
    n = x2d.shape[0]
    grid = (n // tm,)
    row = lambda i: (i, 0)
    pos = lambda i: (i % pos_tiles, 0)
    return pl.pallas_call(
        _inproj_kernel,
        grid=grid,
        in_specs=[
            pl.BlockSpec((tm, D_MODEL), row),
            _const_spec((1, D_MODEL)),
            _const_spec((D_MODEL, D_IN)),
            _const_spec((1, 2 * C_CONV)),
            _const_spec((1, Q_W)),
            _const_spec((1, KV_W)),
            pl.BlockSpec((tm, LANES), pos),
            pl.BlockSpec((tm, LANES), pos),
            _const_spec((MXU_N, MXU_N)),
        ],
        out_specs=[
            pl.BlockSpec((tm, Q_W), row),
            pl.BlockSpec((tm, KV_W), row),
            pl.BlockSpec((tm, KV_W), row),
            pl.BlockSpec((tm, 2 * KV_W), row),
            pl.BlockSpec((tm, 2 * KV_W), row),
            pl.BlockSpec((tm, C_CONV), row),
            pl.BlockSpec((tm, 2 * D_MODEL), row),
        ],
        out_shape=[
            jax.ShapeDtypeStruct((n, Q_W), BF16),
            jax.ShapeDtypeStruct((n, KV_W), F32),
            jax.ShapeDtypeStruct((n, KV_W), F32),
            jax.ShapeDtypeStruct((n, 2 * KV_W), BF16),
            jax.ShapeDtypeStruct((n, 2 * KV_W), BF16),
            jax.ShapeDtypeStruct((n, C_CONV), F32),
            jax.ShapeDtypeStruct((n, 2 * D_MODEL), BF16),
        ],
        scratch_shapes=[pltpu.VMEM((tm, D_MODEL), BF16)],
        compiler_params=pltpu.CompilerParams(dimension_semantics=("arbitrary",),
                                             vmem_limit_bytes=VMEM_LIMIT),
        name="in_proj",
    )(x2d, ln1, w_in, b_glu, qg, kg, cos, sin, seg)


def _attn_kernel(sinks_ref, q_ref, kp_ref, kc_ref, vp_ref, vc_ref, o_ref):
    step = pl.program_id(1)
    bq = WINDOW
    nk = 2 * bq
    qrow = lax.broadcasted_iota(jnp.int32, (bq, nk), 0)
    col = lax.broadcasted_iota(jnp.int32, (bq, nk), 1)
    band = (col > qrow) & (col <= qrow + WINDOW)
    base_first = jnp.where(band & ((col >= bq) | (step > 0)), 0.0, NEG)
    base_rest = jnp.where(band, 0.0, NEG)
    lane = lax.broadcasted_iota(jnp.int32, (bq, LANES), 1)
    sink_slot = lane == 0
    low = lane < HEAD_DIM
    first_row = lax.broadcasted_iota(jnp.int32, (BF16_ROWS, LANES), 0) == 0
    ones = jnp.ones((nk, LANES), BF16)

    def window(prev_ref, p0, cur_ref, c0, lanes):
        head = prev_ref[p0:p0 + BF16_ROWS, lanes]
        head = jnp.where(first_row, jnp.zeros_like(head), head)
        return jnp.concatenate([head, prev_ref[p0 + BF16_ROWS:p0 + bq, lanes], cur_ref[c0:c0 + bq, lanes]],
                               axis=0)

    for sub in range(q_ref.shape[0] // bq):
        rows = slice(sub * bq, (sub + 1) * bq)
        base = base_first if sub == 0 else base_rest
        k_prev, v_prev, p0 = (kp_ref, vp_ref, 0) if sub == 0 else (kc_ref, vc_ref, (sub - 1) * bq)
        for h in range(N_KV_HEADS):
            lanes = slice(h * LANES, (h + 1) * LANES)
            kmat = window(k_prev, p0, kc_ref, sub * bq, lanes)
            vmat = window(v_prev, p0, vc_ref, sub * bq, lanes)
            vrhs = jnp.concatenate([vmat, ones], axis=1)
            q_parts = []
            bias_parts = []
            for g in range(GROUP):
                qc = q_ref[rows, (2 * h + g // 2) * LANES:(2 * h + g // 2 + 1) * LANES]
                zero = jnp.zeros_like(qc)
                q_parts.append(jnp.where(low, qc, zero) if g % 2 == 0 else jnp.where(low, zero, qc))
                sink = sinks_ref[GROUP * h + g] * LOG2E
                bias_parts.append(jnp.concatenate(
                    [jnp.where(sink_slot, sink, base[:, 0:LANES]), base[:, LANES:]], axis=1))
            s = _dot_nt(jnp.concatenate(q_parts, axis=0), kmat) + jnp.concatenate(bias_parts, axis=0)
            m = jnp.max(s, axis=-1, keepdims=True)
            e = jnp.exp2(s - m).astype(BF16)
            pvd = _dot(e, vrhs)
            o = pvd[:, 0:LANES] / pvd[:, LANES:2 * LANES]
            for half in range(2):
                r0 = 2 * half * bq
                o_ref[rows, (2 * h + half) * LANES:(2 * h + half + 1) * LANES] = jnp.where(
                    low, o[r0:r0 + bq], o[r0 + bq:r0 + 2 * bq]).astype(BF16)


ATTN_BLOCKS = 4


def _attention(q, kd, vd, sinks, batch, seq):
    bq = WINDOW
    kvw = N_KV_HEADS * LANES
    nb = seq // bq
    ns = nb // ATTN_BLOCKS
    cur = lambda b, i: (b * ns + i, 0)
    prev = lambda b, i: (b * nb + jnp.maximum(i * ATTN_BLOCKS - 1, 0), 0)
    return pl.pallas_call(
        _attn_kernel,
        grid=(batch, ns),
        in_specs=[
            pl.BlockSpec(memory_space=pltpu.SMEM),
            pl.BlockSpec((ATTN_BLOCKS * bq, Q_W), cur),
            pl.BlockSpec((bq, kvw), prev),
            pl.BlockSpec((ATTN_BLOCKS * bq, kvw), cur),
            pl.BlockSpec((bq, kvw), prev),
            pl.BlockSpec((ATTN_BLOCKS * bq, kvw), cur),
        ],
        out_specs=pl.BlockSpec((ATTN_BLOCKS * bq, Q_W), cur),
        out_shape=jax.ShapeDtypeStruct((batch * seq, Q_W), BF16),
        compiler_params=pltpu.CompilerParams(dimension_semantics=("arbitrary", "arbitrary")),
        name="swa_attention",
    )(sinks, q, kd, kd, vd, vd)


def _sample_attn_kernel(sinks_ref, qb_ref, knew_ref, vnew_ref, ck_ref, cv_ref,
                        nk_ref, nv_ref, o_ref):
    nb, wb, _ = ck_ref.shape
    sink = sinks_ref[...] * LOG2E
    for b in range(nb):
        nk_ref[b, 0:wb - 1, :] = ck_ref[b, 1:wb, :]
        nk_ref[b, wb - 1:wb, :] = knew_ref[b]
        nv_ref[b, 0:wb - 1, :] = cv_ref[b, 1:wb, :]
        nv_ref[b, wb - 1:wb, :] = vnew_ref[b]
        keys = nk_ref[b].astype(BF16)
        vals = nv_ref[b].astype(BF16)
        s = _dot_nt(qb_ref[b], keys)
        m = jnp.maximum(jnp.max(s, axis=-1, keepdims=True), sink)
        e = jnp.exp2(s - m)
        den = jnp.sum(e, axis=-1, keepdims=True) + jnp.exp2(sink - m)
        o_ref[b] = _dot(e.astype(BF16), vals) / den


def _sample_attention(qb, k_new, v_new, cache_k, cache_v, sinks_col, nb):
    bd, wb, _ = cache_k.shape
    blk3 = lambda i: (i, 0, 0)
    return pl.pallas_call(
        _sample_attn_kernel,
        grid=(bd // nb,),
        in_specs=[
            _const_spec((N_HEADS, 1)),
            pl.BlockSpec((nb, N_HEADS, KV_W), blk3),
            pl.BlockSpec((nb, 1, KV_W), blk3),
            pl.BlockSpec((nb, 1, KV_W), blk3),
            pl.BlockSpec((nb, wb, KV_W), blk3),
            pl.BlockSpec((nb, wb, KV_W), blk3),
        ],
        out_specs=[
            pl.BlockSpec((nb, wb, KV_W), blk3),
            pl.BlockSpec((nb, wb, KV_W), blk3),
            pl.BlockSpec((nb, N_HEADS, KV_W), blk3),
        ],
        out_shape=[
            jax.ShapeDtypeStruct((bd, wb, KV_W), F32),
            jax.ShapeDtypeStruct((bd, wb, KV_W), F32),
            jax.ShapeDtypeStruct((bd, N_HEADS, KV_W), F32),
        ],
        compiler_params=pltpu.CompilerParams(dimension_semantics=("arbitrary",)),
        name="sample_attention",
    )(sinks_col, qb, k_new, v_new, cache_k, cache_v)


def _sample_conv_kernel(a_ref, st_ref, dw_ref, dwb_ref, y_ref, ns_ref):
    nb, hist, _ = st_ref.shape
    w_hist = dw_ref[0:hist, :]
    w_last = dw_ref[hist:hist + 1, :]
    for b in range(nb):
        a_new = a_ref[b]
        y_ref[b] = (jnp.sum(st_ref[b] * w_hist, axis=0, keepdims=True)
                    + a_new * w_last + dwb_ref[...])
        ns_ref[b, 0:hist - 1, :] = st_ref[b, 1:hist, :]
        ns_ref[b, hist - 1:hist, :] = a_new


SAMPLE_CONV_ROWS = 32


def _sample_conv(a_new, state, conv_dw, conv_dw_b, nb):
    bd, hist, c = state.shape
    nb = min(nb, bd)
    blk3 = lambda i: (i, 0, 0)
    return pl.pallas_call(
        _sample_conv_kernel,
        grid=(bd // nb,),
        in_specs=[
            pl.BlockSpec((nb, 1, c), blk3),
            pl.BlockSpec((nb, hist, c), blk3),
            _const_spec((CONV_W, c)),
            _const_spec((1, c)),
        ],
        out_specs=[
            pl.BlockSpec((nb, 1, c), blk3),
            pl.BlockSpec((nb, hist, c), blk3),
        ],
        out_shape=[
            jax.ShapeDtypeStruct((bd, 1, c), F32),
            jax.ShapeDtypeStruct((bd, hist, c), F32),
        ],
        compiler_params=pltpu.CompilerParams(dimension_semantics=("arbitrary",)),
        name="sample_conv",
    )(a_new, state, conv_dw, conv_dw_b)


CONV_ROWS = 64
N_CHUNKS = C_CONV // LANES
PITCH = N_CHUNKS + 1


def _hist_store(hist_ref, t0, rows, value):
    for c in range(N_CHUNKS):
        hist_ref[pl.ds(t0 * PITCH + c, rows, stride=PITCH), :] = value[:, c * LANES:(c + 1) * LANES]


def _conv_tile(hist_ref, dw_ref, dwb_ref, y_ref, tm):
    lead = HALO - (CONV_W - 1)
    groups = CONV_ROWS // SUBLANES

    def body(rb, carry):
        r0 = pl.multiple_of(rb * CONV_ROWS, CONV_ROWS)
        for c in range(N_CHUNKS):
            lanes = slice(c * LANES, (c + 1) * LANES)
            taps = [jnp.broadcast_to(dw_ref[j:j + 1, lanes], (SUBLANES, LANES)) for j in range(CONV_W)]
            accs = [jnp.broadcast_to(dwb_ref[:, lanes], (SUBLANES, LANES))] * groups
            for o in range(CONV_ROWS + CONV_W - SUBLANES):
                win = hist_ref[pl.ds((r0 + lead + o) * PITCH + c, SUBLANES, stride=PITCH), :]
                for k in range(groups):
                    j = o - SUBLANES * k
                    if 0 <= j < CONV_W:
                        accs[k] = accs[k] + taps[j] * win
            for k in range(groups):
                y_ref[pl.ds(pl.multiple_of(r0 + SUBLANES * k, SUBLANES), SUBLANES), lanes] = accs[k]
        return carry

    lax.fori_loop(0, tm // CONV_ROWS, body, 0)


LN_ROWS = 64


def _ln_swish(y_ref, act_ref, lng, lnb):
    for r0 in range(0, y_ref.shape[0], LN_ROWS):
        y = y_ref[r0:r0 + LN_ROWS, :]
        d = y - jnp.mean(y, axis=-1, keepdims=True)
        var = jnp.mean(d * d, axis=-1, keepdims=True)
        yn = d * lax.rsqrt(var + EPS) * lng + lnb
        act_ref[r0:r0 + LN_ROWS, :] = (yn * jax.nn.sigmoid(yn)).astype(BF16)


def _mix_project(act, x, attn, gates, wco, bco, wo, wout):
    attn_p = _dot(attn, wo)
    conv_o = _dot(act, wco) + bco
    mixed = (gates[:, 0:D_MODEL].astype(F32) * attn_p
             + gates[:, D_MODEL:2 * D_MODEL].astype(F32) * conv_o)
    return x + _dot(mixed.astype(BF16), wout)


def _mix_prompt_kernel(x_ref, a_ref, halo_ref, attn_ref, g_ref, dw_ref, dwb_ref, lng_ref, lnb_ref,
                       wco_ref, bco_ref, wo_ref, wout_ref, h_ref, hist_ref, y_ref, act_ref):
    tm = x_ref.shape[0]
    seq_start = pl.program_id(1) == 0
    _hist_store(hist_ref, 0, HALO, jnp.where(seq_start, 0.0, halo_ref[...]))
    _hist_store(hist_ref, HALO, tm, a_ref[...])
    _conv_tile(hist_ref, dw_ref, dwb_ref, y_ref, tm)
    _ln_swish(y_ref, act_ref, lng_ref[...], lnb_ref[...])
    h_ref[...] = _mix_project(act_ref[...], x_ref[...], attn_ref[...], g_ref[...],
                              wco_ref[...], bco_ref[...], wo_ref[...], wout_ref[...])


def _mix_sample_kernel(x_ref, y_ref, attn_ref, g_ref, lng_ref, lnb_ref,
                       wco_ref, bco_ref, wo_ref, wout_ref, h_ref, act_ref):
    _ln_swish(y_ref, act_ref, lng_ref[...], lnb_ref[...])
    h_ref[...] = _mix_project(act_ref[...], x_ref[...], attn_ref[...], g_ref[...],
                              wco_ref[...], bco_ref[...], wo_ref[...], wout_ref[...])


def _mix_prompt(x2d, a, attn, gates, batch, seq, tm, dw, dwb, lng, lnb, wco, bco, wo, wout):
    nt = seq // tm
    row = lambda b, i: (b * nt + i, 0)
    halo = lambda b, i: (b * (seq // HALO) + jnp.maximum(i * (tm // HALO) - 1, 0), 0)
    sq = (D_MODEL, D_MODEL)
    return pl.pallas_call(
        _mix_prompt_kernel,
        grid=(batch, nt),
        in_specs=[
            pl.BlockSpec((tm, D_MODEL), row),
            pl.BlockSpec((tm, C_CONV), row),
            pl.BlockSpec((HALO, C_CONV), halo),
            pl.BlockSpec((tm, Q_W), row),
            pl.BlockSpec((tm, 2 * D_MODEL), row),
            _const_spec((CONV_W, C_CONV)),
            _const_spec((1, C_CONV)),
            _const_spec((1, C_CONV)),
            _const_spec((1, C_CONV)),
            _const_spec(sq),
            _const_spec((1, D_MODEL)),
            _const_spec(sq),
            _const_spec(sq),
        ],
        out_specs=pl.BlockSpec((tm, D_MODEL), row),
        out_shape=jax.ShapeDtypeStruct((batch * seq, D_MODEL), F32),
        scratch_shapes=[pltpu.VMEM(((HALO + tm) * PITCH, LANES), F32), pltpu.VMEM((tm, C_CONV), F32),
                        pltpu.VMEM((tm, C_CONV), BF16)],
        compiler_params=pltpu.CompilerParams(dimension_semantics=("arbitrary", "arbitrary"),
                                             vmem_limit_bytes=VMEM_LIMIT),
        name="mix_prompt",
    )(x2d, a, a, attn, gates, dw, dwb, lng, lnb, wco, bco, wo, wout)


def _mix_sample(x2d, y, attn, gates, lng, lnb, wco, bco, wo, wout):
    n = x2d.shape[0]
    sq = (D_MODEL, D_MODEL)
    full = lambda w: pl.BlockSpec((n, w), lambda i: (0, 0))
    return pl.pallas_call(
        _mix_sample_kernel,
        grid=(1,),
        in_specs=[full(D_MODEL), full(C_CONV), full(Q_W), full(2 * D_MODEL),
                  _const_spec((1, C_CONV)), _const_spec((1, C_CONV)),
                  _const_spec(sq), _const_spec((1, D_MODEL)), _const_spec(sq), _const_spec(sq)],
        out_specs=full(D_MODEL),
        out_shape=jax.ShapeDtypeStruct((n, D_MODEL), F32),
        scratch_shapes=[pltpu.VMEM((n, C_CONV), BF16)],
        compiler_params=pltpu.CompilerParams(dimension_semantics=("arbitrary",),
                                             vmem_limit_bytes=VMEM_LIMIT),
        name="mix_sample",
    )(x2d, y, attn, gates, lng, lnb, wco, bco, wo, wout)


FF_CHUNK = 1024


def _ffn_kernel(h_ref, p_ref, ln2_ref, w1_ref, w2_ref, lnp_ref, wg_ref, wp_ref, o_ref, u_ref, acc_ref):
    h = h_ref[...]
    u_ref[...] = (h * ln2_ref[...]).astype(BF16)
    rs = _rms_scale(h)
    for c0 in range(0, D_FF, FF_CHUNK):
        hid = jnp.maximum(_dot(u_ref[...], w1_ref[:, c0:c0 + FF_CHUNK]), 0.0)
        part = _dot((hid * hid).astype(BF16), w2_ref[c0:c0 + FF_CHUNK, :])
        if c0 == 0:
            acc_ref[...] = part
        else:
            acc_ref[...] += part
    h2 = h + acc_ref[...] * (rs * rs)
    un = (h2 * lnp_ref[...]).astype(BF16)
    gate = jax.nn.sigmoid(_dot(un, wg_ref[...]) * _rms_scale(h2))
    o_ref[...] = h2 + gate * _dot(p_ref[...].astype(BF16), wp_ref[...])


def _ffn(h, p2d, tm, ln2, w1, w2, lnp, wg, wp):
    n = h.shape[0]
    d_ple = p2d.shape[1]
    row = lambda i: (i, 0)
    return pl.pallas_call(
        _ffn_kernel,
        grid=(n // tm,),
        in_specs=[
            pl.BlockSpec((tm, D_MODEL), row),
            pl.BlockSpec((tm, d_ple), row),
            _const_spec((1, D_MODEL)),
            _const_spec((D_MODEL, D_FF)),
            _const_spec((D_FF, D_MODEL)),
            _const_spec((1, D_MODEL)),
            _const_spec((D_MODEL, D_MODEL)),
            _const_spec((d_ple, D_MODEL)),
        ],
        out_specs=pl.BlockSpec((tm, D_MODEL), row),
        out_shape=jax.ShapeDtypeStruct((n, D_MODEL), F32),
        scratch_shapes=[pltpu.VMEM((tm, D_MODEL), BF16), pltpu.VMEM((tm, D_MODEL), F32)],
        compiler_params=pltpu.CompilerParams(dimension_semantics=("arbitrary",),
                                             vmem_limit_bytes=VMEM_LIMIT),
        name="ffn",
    )(h, p2d, ln2, w1, w2, lnp, wg, wp)


def _rope_tables(pos):
    half = HEAD_DIM // 2
    inv = jnp.power(jnp.float32(ROPE_THETA), -jnp.arange(half, dtype=F32) / half)
    ang = pos.astype(F32)[:, None] * inv[None, :]
    cos = jnp.tile(jnp.cos(ang), (1, LANES // half))
    sign = jnp.tile(jnp.concatenate([-jnp.ones((half,), F32), jnp.ones((half,), F32)]), LANES // HEAD_DIM)
    sin = jnp.tile(jnp.sin(ang), (1, LANES // half)) * sign[None, :]
    return cos, sin


def _row(v):
    return v.reshape(1, -1)


def kernel(x_prompt, x_sample, cache_k, cache_v, state_conv, p_prompt, p_sample, ln1, w_in, b_glu, q_norm, k_norm, sinks, w_o_attn, conv_dw, conv_dw_b, conv_ln_g, conv_ln_b, w_conv_out, b_conv_out, w_out, ln2, w_ff1, w_ff2, ln_ple, w_ple_gate, w_ple):
    depth = w_in.shape[0]
    batch, seq, _ = x_prompt.shape
    bd, dseq, _ = x_sample.shape
    wb = cache_k.shape[2]
    hist = state_conv.shape[2]
    assert dseq == 1 and wb == WINDOW and hist == CONV_W - 1 and seq % WINDOW == 0

    tm = 512
    seg = (jnp.arange(MXU_N)[:, None] // HEAD_DIM == jnp.arange(MXU_N)[None, :] // HEAD_DIM).astype(BF16)
    cos_p, sin_p = _rope_tables(jnp.arange(seq, dtype=jnp.int32))
    cos_s, sin_s = _rope_tables(jnp.full((bd,), PAST_LEN, dtype=jnp.int32))
    head_ids = jnp.arange(N_HEADS)

    xp = x_prompt.reshape(batch * seq, D_MODEL)
    xs = x_sample.reshape(bd, D_MODEL)
    outs = [[] for _ in range(6)]
    for i in range(depth):
        win = w_in[i].astype(BF16)
        qg = _row(jnp.tile(q_norm[i], N_HEADS))
        kg = _row(jnp.tile(k_norm[i], N_KV_HEADS))
        wco = w_conv_out[i].astype(BF16)
        wo = w_o_attn[i].astype(BF16)
        wout = w_out[i].astype(BF16)
        w1 = w_ff1[i].astype(BF16)
        w2 = w_ff2[i].astype(BF16)
        wg = w_ple_gate[i].astype(BF16)
        wp = w_ple[i].astype(BF16)
        inproj_w = (_row(ln1[i]), win, _row(b_glu[i]), qg, kg, seg)
        post_w = (_row(conv_ln_g[i]), _row(conv_ln_b[i]), wco, _row(b_conv_out[i]), wo, wout)
        ffn_w = (_row(ln2[i]), w1, w2, _row(ln_ple[i]), wg, wp)

        q, k, v, kd, vd, a, g = _inproj(xp, cos_p, sin_p, seq // tm, tm, *inproj_w)
        attn = _attention(q, kd, vd, sinks[i], batch, seq)
        h = _mix_prompt(xp, a, attn, g, batch, seq, tm, conv_dw[i], _row(conv_dw_b[i]), *post_w)
        xp = _ffn(h, p_prompt[i].reshape(batch * seq, -1), tm, *ffn_w)
        outs[0].append(k.reshape(batch, seq, N_KV_HEADS, HEAD_DIM)[:, seq - WINDOW:])
        outs[1].append(v.reshape(batch, seq, N_KV_HEADS, HEAD_DIM)[:, seq - WINDOW:])
        outs[2].append(a.reshape(batch, seq, C_CONV)[:, seq - hist:])

        qs, ks, vs, _, _, a_s, gs = _inproj(xs, cos_s, sin_s, 1, bd, *inproj_w)
        qb = (jnp.zeros((bd, N_HEADS, N_KV_HEADS, HEAD_DIM), BF16)
              .at[:, head_ids, head_ids // GROUP].set(qs.reshape(bd, N_HEADS, HEAD_DIM))
              .reshape(bd, N_HEADS, KV_W))
        nk, nv, ao = _sample_attention(qb, ks.reshape(bd, 1, KV_W), vs.reshape(bd, 1, KV_W),
                                       cache_k[i].reshape(bd, wb, KV_W), cache_v[i].reshape(bd, wb, KV_W),
                                       sinks[i].reshape(N_HEADS, 1), 8)
        attn_s = (ao.reshape(bd, N_HEADS, N_KV_HEADS, HEAD_DIM)[:, head_ids, head_ids // GROUP]
                  .reshape(bd, Q_W).astype(BF16))
        y_s, ns = _sample_conv(a_s.reshape(bd, 1, C_CONV), state_conv[i], conv_dw[i], _row(conv_dw_b[i]),
                               SAMPLE_CONV_ROWS)
        hs = _mix_sample(xs, y_s.reshape(bd, C_CONV), attn_s, gs, *post_w)
        xs = _ffn(hs, p_sample[i].reshape(bd, -1), bd, *ffn_w)
        outs[3].append(nk.reshape(bd, wb, N_KV_HEADS, HEAD_DIM))
        outs[4].append(nv.reshape(bd, wb, N_KV_HEADS, HEAD_DIM))
        outs[5].append(ns)

    return (xp.reshape(batch, seq, D_MODEL), xs.reshape(bd, dseq, D_MODEL),
            jnp.stack(outs[0]), jnp.stack(outs[1]), jnp.stack(outs[2]),
            jnp.stack(outs[3]), jnp.stack(outs[4]), jnp.stack(outs[5]))
```

```python
import functools

import jax
import jax.numpy as jnp
from jax import lax
from jax.experimental import pallas as pl
from jax.experimental.pallas import tpu as pltpu

D_MODEL = 1024
N_HEADS = 16
N_KV_HEADS = 4
HEAD_DIM = 64
GROUP = N_HEADS // N_KV_HEADS
WINDOW = 128
ROPE_THETA = 10000.0
PAST_LEN = 16384
C_CONV = D_MODEL
CONV_W = 31
D_FF = 4 * D_MODEL
EPS = 1e-6
NEG = -1e30
Q_W = N_HEADS * HEAD_DIM
KV_W = N_KV_HEADS * HEAD_DIM
D_IN = Q_W + 2 * KV_W + 2 * C_CONV + 2 * D_MODEL

LOG2E = 1.4426950408889634
Q_SCALE = HEAD_DIM ** -0.5 * LOG2E

LANES = 128
SUBLANES = 8
BF16_ROWS = 16
MXU_N = 256
HALO = 32
VMEM_LIMIT = 56 * 1024 * 1024

F32 = jnp.float32
BF16 = jnp.bfloat16


def _dot(a, b):
    return jnp.dot(a, b, preferred_element_type=F32)


def _dot_nt(a, b):
    return lax.dot_general(a, b, (((1,), (1,)), ((), ())), preferred_element_type=F32)


def _const_spec(shape):
    nd = len(shape)
    return pl.BlockSpec(shape, lambda *_: (0,) * nd, pipeline_mode=pl.Buffered(1))


def _rms_scale(x):
    return lax.rsqrt(jnp.mean(x * x, axis=-1, keepdims=True) + EPS)


def _head_norm_rope(z, gain, seg, cos, sin_signed, scale):
    tm, width = z.shape
    outs = []
    lane = lax.broadcasted_iota(jnp.int32, (tm, LANES), 1)
    first_half = (lane & (HEAD_DIM - 1)) < (HEAD_DIM // 2)
    for c0 in range(0, width, MXU_N):
        zc = z[:, c0:c0 + MXU_N]
        ss = _dot((zc * zc).astype(BF16), seg)
        zn = zc * lax.rsqrt(ss * (1.0 / HEAD_DIM) + EPS) * gain[:, c0:c0 + MXU_N]
        for l0 in range(0, MXU_N, LANES):
            xc = zn[:, l0:l0 + LANES]
            partner = jnp.where(first_half,
                                pltpu.roll(xc, LANES - HEAD_DIM // 2, axis=1),
                                pltpu.roll(xc, HEAD_DIM // 2, axis=1))
            outs.append((xc * cos + partner * sin_signed) * scale)
    return outs


def _store_head_pairs(dup_ref, i, xc):
    low = lax.broadcasted_iota(jnp.int32, xc.shape, 1) < HEAD_DIM
    rolled = pltpu.roll(xc, HEAD_DIM, axis=1)
    dup_ref[:, (2 * i) * LANES:(2 * i + 1) * LANES] = jnp.where(low, xc, rolled).astype(BF16)
    dup_ref[:, (2 * i + 1) * LANES:(2 * i + 2) * LANES] = jnp.where(low, rolled, xc).astype(BF16)


def _inproj_kernel(x_ref, ln1_ref, w_ref, bglu_ref, qg_ref, kg_ref, cos_ref, sin_ref, seg_ref,
                   q_ref, k_ref, v_ref, kd_ref, vd_ref, a_ref, g_ref, u_ref):
    x = x_ref[...]
    u_ref[...] = (x * ln1_ref[...]).astype(BF16)
    rs = _rms_scale(x)
    cos = cos_ref[...]
    sin = sin_ref[...]
    seg = seg_ref[...]

    zq = _dot(u_ref[...], w_ref[:, 0:Q_W]) * rs
    for i, qc in enumerate(_head_norm_rope(zq, qg_ref[...], seg, cos, sin, Q_SCALE)):
        q_ref[:, i * LANES:(i + 1) * LANES] = qc.astype(BF16)

    zkv = _dot(u_ref[...], w_ref[:, Q_W:Q_W + 2 * KV_W]) * rs
    zv = zkv[:, KV_W:2 * KV_W]
    for i, kc in enumerate(_head_norm_rope(zkv[:, 0:KV_W], kg_ref[...], seg, cos, sin, 1.0)):
        k_ref[:, i * LANES:(i + 1) * LANES] = kc
        _store_head_pairs(kd_ref, i, kc)
    v_ref[...] = zv
    for i in range(KV_W // LANES):
        _store_head_pairs(vd_ref, i, zv[:, i * LANES:(i + 1) * LANES])

    glu0 = Q_W + 2 * KV_W
    lin = _dot(u_ref[...], w_ref[:, glu0:glu0 + C_CONV]) * rs + bglu_ref[:, 0:C_CONV]
    gate = _dot(u_ref[...], w_ref[:, glu0 + C_CONV:glu0 + 2 * C_CONV]) * rs + bglu_ref[:, C_CONV:2 * C_CONV]
    a_ref[...] = lin * jax.nn.sigmoid(gate)

    g0 = glu0 + 2 * C_CONV
    for c0 in range(0, 2 * D_MODEL, D_MODEL):
        g_ref[:, c0:c0 + D_MODEL] = jax.nn.sigmoid(
            _dot(u_ref[...], w_ref[:, g0 + c0:g0 + c0 + D_MODEL]) * rs).astype(BF16)


def _inproj(x2d, cos, sin, pos_tiles, tm, ln1, w_in, b_glu, qg, kg, seg):
    n = x2d.shape[0]
    grid = (n // tm,)
    row = lambda i: (i, 0)
    pos = lambda i: (i % pos_tiles, 0)
    return pl.pallas_call(
        _inproj_kernel,
        grid=grid,
        in_specs=[
            pl.BlockSpec((tm, D_MODEL), row),
            _const_spec((1, D_MODEL)),
            _const_spec((D_MODEL, D_IN)),
            _const_spec((1, 2 * C_CONV)),
            _const_spec((1, Q_W)),
            _const_spec((1, KV_W)),
            pl.BlockSpec((tm, LANES), pos),
            pl.BlockSpec((tm, LANES), pos),
            _const_spec((MXU_N, MXU_N)),
        ],
        out_specs=[
            pl.BlockSpec((tm, Q_W), row),
            pl.BlockSpec((tm, KV_W), row),
            pl.BlockSpec((tm, KV_W), row),
            pl.BlockSpec((tm, 2 * KV_W), row),
            pl.BlockSpec((tm, 2 * KV_W), row),
            pl.BlockSpec((tm, C_CONV), row),
            pl.BlockSpec((tm, 2 * D_MODEL), row),
        ],
        out_shape=[
            jax.ShapeDtypeStruct((n, Q_W), BF16),
            jax.ShapeDtypeStruct((n, KV_W), F32),
            jax.ShapeDtypeStruct((n, KV_W), F32),
            jax.ShapeDtypeStruct((n, 2 * KV_W), BF16),
            jax.ShapeDtypeStruct((n, 2 * KV_W), BF16),
            jax.ShapeDtypeStruct((n, C_CONV), F32),
            jax.ShapeDtypeStruct((n, 2 * D_MODEL), BF16),
        ],
        scratch_shapes=[pltpu.VMEM((tm, D_MODEL), BF16)],
        compiler_params=pltpu.CompilerParams(dimension_semantics=("arbitrary",),
                                             vmem_limit_bytes=VMEM_LIMIT),
        name="in_proj",
    )(x2d, ln1, w_in, b_glu, qg, kg, cos, sin, seg)


def _attn_kernel(sinks_ref, q_ref, kp_ref, kc_ref, vp_ref, vc_ref, o_ref):
    step = pl.program_id(1)
    bq = WINDOW
    nk = 2 * bq
    qrow = lax.broadcasted_iota(jnp.int32, (bq, nk), 0)
    col = lax.broadcasted_iota(jnp.int32, (bq, nk), 1)
    band = (col > qrow) & (col <= qrow + WINDOW)
    base_first = jnp.where(band & ((col >= bq) | (step > 0)), 0.0, NEG)
    base_rest = jnp.where(band, 0.0, NEG)
    lane = lax.broadcasted_iota(jnp.int32, (bq, LANES), 1)
    sink_slot = lane == 0
    low = lane < HEAD_DIM
    first_row = lax.broadcasted_iota(jnp.int32, (BF16_ROWS, LANES), 0) == 0
    ones = jnp.ones((nk, LANES), BF16)

    def window(prev_ref, p0, cur_ref, c0, lanes):
        head = prev_ref[p0:p0 + BF16_ROWS, lanes]
        head = jnp.where(first_row, jnp.zeros_like(head), head)
        return jnp.concatenate([head, prev_ref[p0 + BF16_ROWS:p0 + bq, lanes], cur_ref[c0:c0 + bq, lanes]],
                               axis=0)

    for sub in range(q_ref.shape[0] // bq):
        rows = slice(sub * bq, (sub + 1) * bq)
        base = base_first if sub == 0 else base_rest
        k_prev, v_prev, p0 = (kp_ref, vp_ref, 0) if sub == 0 else (kc_ref, vc_ref, (sub - 1) * bq)
        for h in range(N_KV_HEADS):
            lanes = slice(h * LANES, (h + 1) * LANES)
            kmat = window(k_prev, p0, kc_ref, sub * bq, lanes)
            vmat = window(v_prev, p0, vc_ref, sub * bq, lanes)
            vrhs = jnp.concatenate([vmat, ones], axis=1)
            q_parts = []
            bias_parts = []
            for g in range(GROUP):
                qc = q_ref[rows, (2 * h + g // 2) * LANES:(2 * h + g // 2 + 1) * LANES]
                zero = jnp.zeros_like(qc)
                q_parts.append(jnp.where(low, qc, zero) if g % 2 == 0 else jnp.where(low, zero, qc))
                sink = sinks_ref[GROUP * h + g] * LOG2E
                bias_parts.append(jnp.concatenate(
                    [jnp.where(sink_slot, sink, base[:, 0:LANES]), base[:, LANES:]], axis=1))
            s = _dot_nt(jnp.concatenate(q_parts, axis=0), kmat) + jnp.concatenate(bias_parts, axis=0)
            m = jnp.max(s, axis=-1, keepdims=True)
            e = jnp.exp2(s - m).astype(BF16)
            pvd = _dot(e, vrhs)
            o = pvd[:, 0:LANES] / pvd[:, LANES:2 * LANES]
            for half in range(2):
                r0 = 2 * half * bq
                o_ref[rows, (2 * h + half) * LANES:(2 * h + half + 1) * LANES] = jnp.where(
                    low, o[r0:r0 + bq], o[r0 + bq:r0 + 2 * bq]).astype(BF16)


ATTN_BLOCKS = 8


def _attention(q, kd, vd, sinks, batch, seq):
    bq = WINDOW
    kvw = N_KV_HEADS * LANES
    nb = seq // bq
    ns = nb // ATTN_BLOCKS
    cur = lambda b, i: (b * ns + i, 0)
    prev = lambda b, i: (b * nb + jnp.maximum(i * ATTN_BLOCKS - 1, 0), 0)
    return pl.pallas_call(
        _attn_kernel,
        grid=(batch, ns),
        in_specs=[
            pl.BlockSpec(memory_space=pltpu.SMEM),
            pl.BlockSpec((ATTN_BLOCKS * bq, Q_W), cur),
            pl.BlockSpec((bq, kvw), prev),
            pl.BlockSpec((ATTN_BLOCKS * bq, kvw), cur),
            pl.BlockSpec((bq, kvw), prev),
            pl.BlockSpec((ATTN_BLOCKS * bq, kvw), cur),
        ],
        out_specs=pl.BlockSpec((ATTN_BLOCKS * bq, Q_W), cur),
        out_shape=jax.ShapeDtypeStruct((batch * seq, Q_W), BF16),
        compiler_params=pltpu.CompilerParams(dimension_semantics=("arbitrary", "arbitrary")),
        name="swa_attention",
    )(sinks, q, kd, kd, vd, vd)


def _sample_attn_kernel(sinks_ref, qb_ref, knew_ref, vnew_ref, ck_ref, cv_ref,
                        nk_ref, nv_ref, o_ref):
    nb, wb, _ = ck_ref.shape
    sink = sinks_ref[...] * LOG2E
    for b in range(nb):
        nk_ref[b, 0:wb - 1, :] = ck_ref[b, 1:wb, :]
        nk_ref[b, wb - 1:wb, :] = knew_ref[b]
        nv_ref[b, 0:wb - 1, :] = cv_ref[b, 1:wb, :]
        nv_ref[b, wb - 1:wb, :] = vnew_ref[b]
        keys = nk_ref[b].astype(BF16)
        vals = nv_ref[b].astype(BF16)
        s = _dot_nt(qb_ref[b], keys)
        m = jnp.maximum(jnp.max(s, axis=-1, keepdims=True), sink)
        e = jnp.exp2(s - m)
        den = jnp.sum(e, axis=-1, keepdims=True) + jnp.exp2(sink - m)
        o_ref[b] = _dot(e.astype(BF16), vals) / den


def _sample_attention(qb, k_new, v_new, cache_k, cache_v, sinks_col, nb):
    bd, wb, _ = cache_k.shape
    blk3 = lambda i: (i, 0, 0)
    return pl.pallas_call(
        _sample_attn_kernel,
        grid=(bd // nb,),
        in_specs=[
            _const_spec((N_HEADS, 1)),
            pl.BlockSpec((nb, N_HEADS, KV_W), blk3),
            pl.BlockSpec((nb, 1, KV_W), blk3),
            pl.BlockSpec((nb, 1, KV_W), blk3),
            pl.BlockSpec((nb, wb, KV_W), blk3),
            pl.BlockSpec((nb, wb, KV_W), blk3),
        ],
        out_specs=[
            pl.BlockSpec((nb, wb, KV_W), blk3),
            pl.BlockSpec((nb, wb, KV_W), blk3),
            pl.BlockSpec((nb, N_HEADS, KV_W), blk3),
        ],
        out_shape=[
            jax.ShapeDtypeStruct((bd, wb, KV_W), F32),
            jax.ShapeDtypeStruct((bd, wb, KV_W), F32),
            jax.ShapeDtypeStruct((bd, N_HEADS, KV_W), F32),
        ],
        compiler_params=pltpu.CompilerParams(dimension_semantics=("arbitrary",)),
        name="sample_attention",
    )(sinks_col, qb, k_new, v_new, cache_k, cache_v)


def _sample_conv_kernel(a_ref, st_ref, dw_ref, dwb_ref, y_ref, ns_ref):
    nb, hist, _ = st_ref.shape
    w_hist = dw_ref[0:hist, :]
    w_last = dw_ref[hist:hist + 1, :]
    for b in range(nb):
        a_new = a_ref[b]
        y_ref[b] = (jnp.sum(st_ref[b] * w_hist, axis=0, keepdims=True)
                    + a_new * w_last + dwb_ref[...])
        ns_ref[b, 0:hist - 1, :] = st_ref[b, 1:hist, :]
        ns_ref[b, hist - 1:hist, :] = a_new


SAMPLE_CONV_ROWS = 32


def _sample_conv(a_new, state, conv_dw, conv_dw_b, nb):
    bd, hist, c = state.shape
    nb = min(nb, bd)
    blk3 = lambda i: (i, 0, 0)
    return pl.pallas_call(
        _sample_conv_kernel,
        grid=(bd // nb,),
        in_specs=[
            pl.BlockSpec((nb, 1, c), blk3),
            pl.BlockSpec((nb, hist, c), blk3),
            _const_spec((CONV_W, c)),
            _const_spec((1, c)),
        ],
        out_specs=[
            pl.BlockSpec((nb, 1, c), blk3),
            pl.BlockSpec((nb, hist, c), blk3),
        ],
        out_shape=[
            jax.ShapeDtypeStruct((bd, 1, c), F32),
            jax.ShapeDtypeStruct((bd, hist, c), F32),
        ],
        compiler_params=pltpu.CompilerParams(dimension_semantics=("arbitrary",)),
        name="sample_conv",
    )(a_new, state, conv_dw, conv_dw_b)


CONV_ROWS = 64
N_CHUNKS = C_CONV // LANES
PITCH = N_CHUNKS + 1


def _hist_store(hist_ref, t0, rows, value):
    for c in range(N_CHUNKS):
        hist_ref[pl.ds(t0 * PITCH + c, rows, stride=PITCH), :] = value[:, c * LANES:(c + 1) * LANES]


def _conv_tile(hist_ref, dw_ref, dwb_ref, y_ref, tm):
    lead = HALO - (CONV_W - 1)
    groups = CONV_ROWS // SUBLANES

    def body(rb, carry):
        r0 = pl.multiple_of(rb * CONV_ROWS, CONV_ROWS)
        for c in range(N_CHUNKS):
            lanes = slice(c * LANES, (c + 1) * LANES)
            taps = [jnp.broadcast_to(dw_ref[j:j + 1, lanes], (SUBLANES, LANES)) for j in range(CONV_W)]
            accs = [jnp.broadcast_to(dwb_ref[:, lanes], (SUBLANES, LANES))] * groups
            for o in range(CONV_ROWS + CONV_W - SUBLANES):
                win = hist_ref[pl.ds((r0 + lead + o) * PITCH + c, SUBLANES, stride=PITCH), :]
                for k in range(groups):
                    j = o - SUBLANES * k
                    if 0 <= j < CONV_W:
                        accs[k] = accs[k] + taps[j] * win
            for k in range(groups):
                y_ref[pl.ds(pl.multiple_of(r0 + SUBLANES * k, SUBLANES), SUBLANES), lanes] = accs[k]
        return carry

    lax.fori_loop(0, tm // CONV_ROWS, body, 0)


LN_ROWS = 64


def _ln_swish(y_ref, act_ref, lng, lnb):
    for r0 in range(0, y_ref.shape[0], LN_ROWS):
        y = y_ref[r0:r0 + LN_ROWS, :]
        d = y - jnp.mean(y, axis=-1, keepdims=True)
        var = jnp.mean(d * d, axis=-1, keepdims=True)
        yn = d * lax.rsqrt(var + EPS) * lng + lnb
        act_ref[r0:r0 + LN_ROWS, :] = (yn * jax.nn.sigmoid(yn)).astype(BF16)


def _mix_project(act, x, attn, gates, wco, bco, wo, wout):
    attn_p = _dot(attn, wo)
    conv_o = _dot(act, wco) + bco
    mixed = (gates[:, 0:D_MODEL].astype(F32) * attn_p
             + gates[:, D_MODEL:2 * D_MODEL].astype(F32) * conv_o)
    return x + _dot(mixed.astype(BF16), wout)


def _mix_prompt_kernel(x_ref, a_ref, halo_ref, attn_ref, g_ref, dw_ref, dwb_ref, lng_ref, lnb_ref,
                       wco_ref, bco_ref, wo_ref, wout_ref, h_ref, hist_ref, y_ref, act_ref):
    tm = x_ref.shape[0]
    seq_start = pl.program_id(1) == 0
    _hist_store(hist_ref, 0, HALO, jnp.where(seq_start, 0.0, halo_ref[...]))
    _hist_store(hist_ref, HALO, tm, a_ref[...])
    _conv_tile(hist_ref, dw_ref, dwb_ref, y_ref, tm)
    _ln_swish(y_ref, act_ref, lng_ref[...], lnb_ref[...])
    h_ref[...] = _mix_project(act_ref[...], x_ref[...], attn_ref[...], g_ref[...],
                              wco_ref[...], bco_ref[...], wo_ref[...], wout_ref[...])


def _mix_sample_kernel(x_ref, y_ref, attn_ref, g_ref, lng_ref, lnb_ref,
                       wco_ref, bco_ref, wo_ref, wout_ref, h_ref, act_ref):
    _ln_swish(y_ref, act_ref, lng_ref[...], lnb_ref[...])
    h_ref[...] = _mix_project(act_ref[...], x_ref[...], attn_ref[...], g_ref[...],
                              wco_ref[...], bco_ref[...], wo_ref[...], wout_ref[...])


def _mix_prompt(x2d, a, attn, gates, batch, seq, tm, dw, dwb, lng, lnb, wco, bco, wo, wout):
    nt = seq // tm
    row = lambda b, i: (b * nt + i, 0)
    halo = lambda b, i: (b * (seq // HALO) + jnp.maximum(i * (tm // HALO) - 1, 0), 0)
    sq = (D_MODEL, D_MODEL)
    return pl.pallas_call(
        _mix_prompt_kernel,
        grid=(batch, nt),
        in_specs=[
            pl.BlockSpec((tm, D_MODEL), row),
            pl.BlockSpec((tm, C_CONV), row),
            pl.BlockSpec((HALO, C_CONV), halo),
            pl.BlockSpec((tm, Q_W), row),
            pl.BlockSpec((tm, 2 * D_MODEL), row),
            _const_spec((CONV_W, C_CONV)),
            _const_spec((1, C_CONV)),
            _const_spec((1, C_CONV)),
            _const_spec((1, C_CONV)),
            _const_spec(sq),
            _const_spec((1, D_MODEL)),
            _const_spec(sq),
            _const_spec(sq),
        ],
        out_specs=pl.BlockSpec((tm, D_MODEL), row),
        out_shape=jax.ShapeDtypeStruct((batch * seq, D_MODEL), F32),
        scratch_shapes=[pltpu.VMEM(((HALO + tm) * PITCH, LANES), F32), pltpu.VMEM((tm, C_CONV), F32),
                        pltpu.VMEM((tm, C_CONV), BF16)],
        compiler_params=pltpu.CompilerParams(dimension_semantics=("arbitrary", "arbitrary"),
                                             vmem_limit_bytes=VMEM_LIMIT),
        name="mix_prompt",
    )(x2d, a, a, attn, gates, dw, dwb, lng, lnb, wco, bco, wo, wout)


def _mix_sample(x2d, y, attn, gates, lng, lnb, wco, bco, wo, wout):
    n = x2d.shape[0]
    sq = (D_MODEL, D_MODEL)
    full = lambda w: pl.BlockSpec((n, w), lambda i: (0, 0))
    return pl.pallas_call(
        _mix_sample_kernel,
        grid=(1,),
        in_specs=[full(D_MODEL), full(C_CONV), full(Q_W), full(2 * D_MODEL),
                  _const_spec((1, C_CONV)), _const_spec((1, C_CONV)),
                  _const_spec(sq), _const_spec((1, D_MODEL)), _const_spec(sq), _const_spec(sq)],
        out_specs=full(D_MODEL),
        out_shape=jax.ShapeDtypeStruct((n, D_MODEL), F32),
        scratch_shapes=[pltpu.VMEM((n, C_CONV), BF16)],
        compiler_params=pltpu.CompilerParams(dimension_semantics=("arbitrary",),
                                             vmem_limit_bytes=VMEM_LIMIT),
        name="mix_sample",
    )(x2d, y, attn, gates, lng, lnb, wco, bco, wo, wout)


FF_CHUNK = 1024


def _ffn_kernel(h_ref, p_ref, ln2_ref, w1_ref, w2_ref, lnp_ref, wg_ref, wp_ref, o_ref, u_ref, acc_ref):
    h = h_ref[...]
    u_ref[...] = (h * ln2_ref[...]).astype(BF16)
    rs = _rms_scale(h)
    for c0 in range(0, D_FF, FF_CHUNK):
        hid = jnp.maximum(_dot(u_ref[...], w1_ref[:, c0:c0 + FF_CHUNK]), 0.0)
        part = _dot((hid * hid).astype(BF16), w2_ref[c0:c0 + FF_CHUNK, :])
        if c0 == 0:
            acc_ref[...] = part
        else:
            acc_ref[...] += part
    h2 = h + acc_ref[...] * (rs * rs)
    un = (h2 * lnp_ref[...]).astype(BF16)
    gate = jax.nn.sigmoid(_dot(un, wg_ref[...]) * _rms_scale(h2))
    o_ref[...] = h2 + gate * _dot(p_ref[...].astype(BF16), wp_ref[...])


def _ffn(h, p2d, tm, ln2, w1, w2, lnp, wg, wp):
    n = h.shape[0]
    d_ple = p2d.shape[1]
    row = lambda i: (i, 0)
    return pl.pallas_call(
        _ffn_kernel,
        grid=(n // tm,),
        in_specs=[
            pl.BlockSpec((tm, D_MODEL), row),
            pl.BlockSpec((tm, d_ple), row),
            _const_spec((1, D_MODEL)),
            _const_spec((D_MODEL, D_FF)),
            _const_spec((D_FF, D_MODEL)),
            _const_spec((1, D_MODEL)),
            _const_spec((D_MODEL, D_MODEL)),
            _const_spec((d_ple, D_MODEL)),
        ],
        out_specs=pl.BlockSpec((tm, D_MODEL), row),
        out_shape=jax.ShapeDtypeStruct((n, D_MODEL), F32),
        scratch_shapes=[pltpu.VMEM((tm, D_MODEL), BF16), pltpu.VMEM((tm, D_MODEL), F32)],
        compiler_params=pltpu.CompilerParams(dimension_semantics=("arbitrary",),
                                             vmem_limit_bytes=VMEM_LIMIT),
        name="ffn",
    )(h, p2d, ln2, w1, w2, lnp, wg, wp)


def _rope_tables(pos):
    half = HEAD_DIM // 2
    inv = jnp.power(jnp.float32(ROPE_THETA), -jnp.arange(half, dtype=F32) / half)
    ang = pos.astype(F32)[:, None] * inv[None, :]
    cos = jnp.tile(jnp.cos(ang), (1, LANES // half))
    sign = jnp.tile(jnp.concatenate([-jnp.ones((half,), F32), jnp.ones((half,), F32)]), LANES // HEAD_DIM)
    sin = jnp.tile(jnp.sin(ang), (1, LANES // half)) * sign[None, :]
    return cos, sin


def _row(v):
    return v.reshape(1, -1)


def kernel(x_prompt, x_sample, cache_k, cache_v, state_conv, p_prompt, p_sample, ln1, w_in, b_glu, q_norm, k_norm, sinks, w_o_attn, conv_dw, conv_dw_b, conv_ln_g, conv_ln_b, w_conv_out, b_conv_out, w_out, ln2, w_ff1, w_ff2, ln_ple, w_ple_gate, w_ple):
    depth = w_in.shape[0]
    batch, seq, _ = x_prompt.shape
    bd, dseq, _ = x_sample.shape
    wb = cache_k.shape[2]
    hist = state_conv.shape[2]
    assert dseq == 1 and wb == WINDOW and hist == CONV_W - 1 and seq % WINDOW == 0

    tm = 512
    seg = (jnp.arange(MXU_N)[:, None] // HEAD_DIM == jnp.arange(MXU_N)[None, :] // HEAD_DIM).astype(BF16)
    cos_p, sin_p = _rope_tables(jnp.arange(seq, dtype=jnp.int32))
    cos_s, sin_s = _rope_tables(jnp.full((bd,), PAST_LEN, dtype=jnp.int32))
    head_ids = jnp.arange(N_HEADS)

    xp = x_prompt.reshape(batch * seq, D_MODEL)
    xs = x_sample.reshape(bd, D_MODEL)
    outs = [[] for _ in range(6)]
    for i in range(depth):
        win = w_in[i].astype(BF16)
        qg = _row(jnp.tile(q_norm[i], N_HEADS))
        kg = _row(jnp.tile(k_norm[i], N_KV_HEADS))
        wco = w_conv_out[i].astype(BF16)
        wo = w_o_attn[i].astype(BF16)
        wout = w_out[i].astype(BF16)
        w1 = w_ff1[i].astype(BF16)
        w2 = w_ff2[i].astype(BF16)
        wg = w_ple_gate[i].astype(BF16)
        wp = w_ple[i].astype(BF16)
        inproj_w = (_row(ln1[i]), win, _row(b_glu[i]), qg, kg, seg)
        post_w = (_row(conv_ln_g[i]), _row(conv_ln_b[i]), wco, _row(b_conv_out[i]), wo, wout)
        ffn_w = (_row(ln2[i]), w1, w2, _row(ln_ple[i]), wg, wp)

        q, k, v, kd, vd, a, g = _inproj(xp, cos_p, sin_p, seq // tm, tm, *inproj_w)
        attn = _attention(q, kd, vd, sinks[i], batch, seq)
        h = _mix_prompt(xp, a, attn, g, batch, seq, tm, conv_dw[i], _row(conv_dw_b[i]), *post_w)
        xp = _ffn(h, p_prompt[i].reshape(batch * seq, -1), tm, *ffn_w)
        outs[0].append(k.reshape(batch, seq, N_KV_HEADS, HEAD_DIM)[:, seq - WINDOW:])
        outs[1].append(v.reshape(batch, seq, N_KV_HEADS, HEAD_DIM)[:, seq - WINDOW:])
        outs[2].append(a.reshape(batch, seq, C_CONV)[:, seq - hist:])

        qs, ks, vs, _, _, a_s, gs = _inproj(xs, cos_s, sin_s, 1, bd, *inproj_w)
        qb = (jnp.zeros((bd, N_HEADS, N_KV_HEADS, HEAD_DIM), BF16)
              .at[:, head_ids, head_ids // GROUP].set(qs.reshape(bd, N_HEADS, HEAD_DIM))
              .reshape(bd, N_HEADS, KV_W))
        nk, nv, ao = _sample_attention(qb, ks.reshape(bd, 1, KV_W), vs.reshape(bd, 1, KV_W),
                                       cache_k[i].reshape(bd, wb, KV_W), cache_v[i].reshape(bd, wb, KV_W),
                                       sinks[i].reshape(N_HEADS, 1), 8)
        attn_s = (ao.reshape(bd, N_HEADS, N_KV_HEADS, HEAD_DIM)[:, head_ids, head_ids // GROUP]
                  .reshape(bd, Q_W).astype(BF16))
        y_s, ns = _sample_conv(a_s.reshape(bd, 1, C_CONV), state_conv[i], conv_dw[i], _row(conv_dw_b[i]),
                               SAMPLE_CONV_ROWS)
        hs = _mix_sample(xs, y_s.reshape(bd, C_CONV), attn_s, gs, *post_w)
        xs = _ffn(hs, p_sample[i].reshape(bd, -1), bd, *ffn_w)
        outs[3].append(nk.reshape(bd, wb, N_KV_HEADS, HEAD_DIM))
        outs[4].append(nv.reshape(bd, wb, N_KV_HEADS, HEAD_DIM))
        outs[5].append(ns)

    return (xp.reshape(batch, seq, D_MODEL), xs.reshape(bd, dseq, D_MODEL),
            jnp.stack(outs[0]), jnp.stack(outs[1]), jnp.stack(outs[2]),
            jnp.stack(outs[3]), jnp.stack(outs[4]), jnp.stack(outs[5]))
```
